```python
import jax, jax.numpy as jnp
from jax import lax
import numpy as np

D_MODEL = 2048
BATCH = 2
SEQ = 4096
DEPTH = 1

POOL_WIDTH = D_MODEL // 2
POOL_WINDOWS = (2, 4, 8, 16)
N_POOL_GROUPS = len(POOL_WINDOWS)
POOL_GROUP_DIM = POOL_WIDTH // N_POOL_GROUPS
HGRN_WIDTH = D_MODEL - POOL_WIDTH
HGRN_DK = 128
HGRN_HEADS = HGRN_WIDTH // HGRN_DK
HGRN_DV = HGRN_WIDTH // HGRN_HEADS
HGRN_KEY_WIDTH = HGRN_HEADS * HGRN_DK
CHUNK = 64
IN_COLS = POOL_WIDTH + 2 * HGRN_KEY_WIDTH + 2 * HGRN_WIDTH
D_FF = 5632
CONV_WIDTH = 3
ALPHA = (2.0 * DEPTH) ** 0.25
BETA = (8.0 * DEPTH) ** -0.25
LN_EPS = 1e-5
RMS_EPS = 1e-6

kernel_name = "hymba_pool_hgrn2_convffn_deepnorm"


def layer_norm(x, g, b):
    x32 = x.astype(jnp.float32)
    mu = jnp.mean(x32, axis=-1, keepdims=True)
    var = jnp.mean(jnp.square(x32 - mu), axis=-1, keepdims=True)
    y = (x32 - mu) * lax.rsqrt(var + LN_EPS) * g.astype(jnp.float32) + b.astype(jnp.float32)
    return y.astype(x.dtype)


def pool_mixer(u, pool_w, pool_b, pool_scale):
    B, S, _ = u.shape
    u32 = u.astype(jnp.float32).reshape(B, S, N_POOL_GROUPS, POOL_GROUP_DIM)
    cs = jnp.cumsum(u32, axis=1)
    pos = jnp.arange(S)
    pooled = []
    for gi, w in enumerate(POOL_WINDOWS):
        c = cs[:, :, gi]
        c_prev = jnp.pad(c, ((0, 0), (w, 0), (0, 0)))[:, :S]
        cnt = jnp.minimum(pos + 1, w).astype(jnp.float32)[None, :, None]
        pooled.append((c - c_prev) / cnt)
    pooled = jnp.stack(pooled, axis=2) - u32
    y = jnp.einsum('bsgc,gcd->bsgd', pooled.astype(u.dtype), pool_w) + pool_b
    return y.reshape(B, S, POOL_WIDTH) * pool_scale


def hgrn2_mixer(q, f_logit, v, gate, lb, g_norm):
    B, S, _ = q.shape
    H, dk, dv, C = HGRN_HEADS, HGRN_DK, HGRN_DV, CHUNK
    n_chunks = S // C
    lb32 = lb.astype(jnp.float32)
    f = lb32 + (1.0 - lb32) * jax.nn.sigmoid(f_logit.astype(jnp.float32))
    log_f = jnp.log(f)
    k = 1.0 - f

    def to_chunks(t, d):
        return t.astype(jnp.float32).reshape(B, n_chunks, C, H, d).transpose(1, 0, 3, 2, 4)

    qc, kc, gc, vc = to_chunks(q, dk), to_chunks(k, dk), to_chunks(log_f, dk), to_chunks(v, dv)
    causal = jnp.tril(jnp.ones((C, C), dtype=bool))[:, :, None]

    def step(state, inp):
        qt, kt, gt, vt = inp
        b = jnp.cumsum(gt, axis=-2)
        o_inter = jnp.einsum('bhck,bhkv->bhcv', qt * jnp.exp(b), state)
        diff = b[:, :, :, None, :] - b[:, :, None, :, :]
        decay = jnp.exp(jnp.where(causal, diff, -jnp.inf))
        scores = jnp.einsum('bhtk,bhsk,bhtsk->bhts', qt, kt, decay)
        o_intra = jnp.einsum('bhts,bhsv->bhtv', scores, vt)
        b_last = b[:, :, -1:, :]
        new_state = (jnp.exp(b_last[:, :, 0, :])[..., None] * state
                     + jnp.einsum('bhsk,bhsv->bhkv', kt * jnp.exp(b_last - b), vt))
        return new_state, o_inter + o_intra

    s0 = jnp.zeros((B, H, dk, dv), jnp.float32)
    _, o = lax.scan(step, s0, (qc, kc, gc, vc))
    o = o.transpose(1, 0, 3, 2, 4).reshape(B, S, H, dv)
    o = o * lax.rsqrt(jnp.mean(jnp.square(o), axis=-1, keepdims=True) + RMS_EPS)
    o = o.reshape(B, S, H * dv) * g_norm.astype(jnp.float32)
    o = o * jax.nn.silu(gate.astype(jnp.float32))
    return o.astype(q.dtype)


def conv_ffn(h, w_up, conv_w, conv_b, w_down):
    S = h.shape[1]
    u = h @ w_up
    up = jnp.pad(u, ((0, 0), (CONV_WIDTH - 1, 0), (0, 0)))
    uc = conv_b + sum(conv_w[j] * up[:, j:j + S] for j in range(CONV_WIDTH))
    g, val = jnp.split(uc, 2, axis=-1)
    return (jax.nn.silu(g) * val) @ w_down


def setup_inputs(seed: int = 0) -> dict:
    key = jax.random.key(seed)
    ks = jax.random.split(key, 20)
    L = DEPTH
    nrm = lambda k, shape: jax.random.normal(k, shape, jnp.float32)
    return {
        "x": nrm(ks[0], (BATCH, SEQ, D_MODEL)),
        "w_in": nrm(ks[1], (L, D_MODEL, IN_COLS)) * D_MODEL ** -0.5,
        "pool_w": nrm(ks[2], (L, N_POOL_GROUPS, POOL_GROUP_DIM, POOL_GROUP_DIM)) * POOL_GROUP_DIM ** -0.5,
        "pool_b": 0.02 * nrm(ks[3], (L, N_POOL_GROUPS, POOL_GROUP_DIM)),
        "pool_scale": 1.0 + 0.1 * nrm(ks[4], (L, POOL_WIDTH)),
        "hgrn_lb_logits": 1.0 + 0.5 * nrm(ks[5], (L + 1, HGRN_KEY_WIDTH)),
        "hgrn_g_norm": 1.0 + 0.02 * nrm(ks[6], (L, HGRN_WIDTH)),
        "w_out": nrm(ks[7], (L, D_MODEL, D_MODEL)) * D_MODEL ** -0.5 * BETA,
        "ln1_g": 1.0 + 0.02 * nrm(ks[8], (L, D_MODEL)),
        "ln1_b": 0.02 * nrm(ks[9], (L, D_MODEL)),
        "w_up": nrm(ks[10], (L, D_MODEL, 2 * D_FF)) * D_MODEL ** -0.5,
        "conv_w": nrm(ks[11], (L, CONV_WIDTH, 2 * D_FF)) * CONV_WIDTH ** -0.5,
        "conv_b": 0.02 * nrm(ks[12], (L, 2 * D_FF)),
        "w_down": nrm(ks[13], (L, D_FF, D_MODEL)) * D_FF ** -0.5 * BETA,
        "ln2_g": 1.0 + 0.02 * nrm(ks[14], (L, D_MODEL)),
        "ln2_b": 0.02 * nrm(ks[15], (L, D_MODEL)),
    }


def reference(x, w_in, pool_w, pool_b, pool_scale, hgrn_lb_logits, hgrn_g_norm, w_out,
              ln1_g, ln1_b, w_up, conv_w, conv_b, w_down, ln2_g, ln2_b):
    lb_all = jnp.cumsum(jax.nn.softmax(hgrn_lb_logits.astype(jnp.float32), axis=0), axis=0)
    for l in range(DEPTH):
        proj = x @ w_in[l]
        o1 = POOL_WIDTH
        o2 = o1 + HGRN_KEY_WIDTH
        o3 = o2 + HGRN_KEY_WIDTH
        o4 = o3 + HGRN_WIDTH
        u_pool = proj[..., :o1]
        q, f_logit, v, gate = proj[..., o1:o2], proj[..., o2:o3], proj[..., o3:o4], proj[..., o4:]
        a_out = pool_mixer(u_pool, pool_w[l], pool_b[l], pool_scale[l])
        b_out = hgrn2_mixer(q, f_logit, v, gate, lb_all[l], hgrn_g_norm[l])
        mix = jnp.concatenate([a_out, b_out], axis=-1) @ w_out[l]
        x = layer_norm(ALPHA * x + mix, ln1_g[l], ln1_b[l])
        x = layer_norm(ALPHA * x + conv_ffn(x, w_up[l], conv_w[l], conv_b[l], w_down[l]), ln2_g[l], ln2_b[l])
    return x
```

```python
import functools

import jax
import jax.numpy as jnp
from jax import lax
from jax.experimental import pallas as pl
from jax.experimental.pallas import tpu as pltpu

D_MODEL = 2048
POOL_WIDTH = 1024
POOL_WINDOWS = (2, 4, 8, 16)
POOL_GROUP_DIM = 256
HGRN_WIDTH = 1024
HGRN_HEADS = 8
HGRN_DK = 128
IN_COLS = 5120
D_FF = 5632
CONV_WIDTH = 3
DEPTH = 1
ALPHA = (2.0 * DEPTH) ** 0.25
LN_EPS = 1e-5
RMS_EPS = 1e-6

V7X_LANES = 128
V7X_SUBLANES = 8
V7X_VMEM_BYTES = 64 * 1024 * 1024

HGRN_BLOCK = 16
POOL_HALO = 16
CONV_HALO = V7X_SUBLANES

BF16 = jnp.bfloat16
F32 = jnp.float32


def _vmem_limit(block_bytes):
    return int(min(V7X_VMEM_BYTES * 7 // 8, 2 * block_bytes + 16 * 1024 * 1024))


def _dot(a, b):
    return jnp.dot(a, b, preferred_element_type=F32)


def _sigmoid(x):
    return 1.0 / (1.0 + jnp.exp(-x))


def _layer_norm(y, g, b):
    mu = jnp.mean(y, axis=-1, keepdims=True)
    yc = y - mu
    var = jnp.mean(yc * yc, axis=-1, keepdims=True)
    return yc * lax.rsqrt(var + LN_EPS) * g + b


def _proj_kernel(x_ref, w_ref, o_ref, xb_ref):
    @pl.when(pl.program_id(1) == 0)
    def _():
        xb_ref[...] = x_ref[...].astype(BF16)

    o_ref[...] = _dot(xb_ref[...], w_ref[...])


def _proj(x2d, w_bf, tm=1024, tn=1024):
    t, d = x2d.shape
    n = w_bf.shape[1]
    blocks = tm * d * 4 + d * tn * 2 + tm * tn * 4
    return pl.pallas_call(
        _proj_kernel,
        grid=(t // tm, n // tn),
        in_specs=[pl.BlockSpec((tm, d), lambda i, j: (i, 0)),
                  pl.BlockSpec((d, tn), lambda i, j: (0, j))],
        out_specs=pl.BlockSpec((tm, tn), lambda i, j: (i, j)),
        out_shape=jax.ShapeDtypeStruct((t, n), F32),
        scratch_shapes=[pltpu.VMEM((tm, d), BF16)],
        compiler_params=pltpu.CompilerParams(
            dimension_semantics=("arbitrary", "arbitrary"),
            vmem_limit_bytes=_vmem_limit(blocks + tm * d * 2)),
        name="proj",
    )(x2d, w_bf)


def _pool_kernel(u_ref, w_ref, pb_ref, ps_ref, o_ref, buf_ref, *, ts):
    i = pl.program_id(1)

    @pl.when(i == 0)
    def _():
        buf_ref[0:POOL_HALO, :] = jnp.zeros((POOL_HALO, POOL_WIDTH), F32)

    @pl.when(i > 0)
    def _():
        buf_ref[0:POOL_HALO, :] = buf_ref[ts:ts + POOL_HALO, :]

    buf_ref[POOL_HALO:POOL_HALO + ts, :] = u_ref[0]

    pos = i * ts + lax.broadcasted_iota(jnp.int32, (ts, POOL_GROUP_DIM), 0)
    for gi, w in enumerate(POOL_WINDOWS):
        cols = slice(gi * POOL_GROUP_DIM, (gi + 1) * POOL_GROUP_DIM)
        cur = buf_ref[POOL_HALO:POOL_HALO + ts, cols]
        win = cur
        for j in range(1, w):
            win = win + buf_ref[POOL_HALO - j:POOL_HALO - j + ts, cols]
        cnt = jnp.minimum(pos + 1, w).astype(F32)
        pooled = win / cnt - cur
        y = _dot(pooled.astype(BF16), w_ref[gi]) + pb_ref[:, cols]
        o_ref[:, cols] = (y * ps_ref[:, cols]).astype(BF16)


def _pool(proj3, pool_w_bf, pool_b, pool_scale, ts=512):
    b, s, _ = proj3.shape
    nt = s // ts
    blocks = ts * POOL_WIDTH * 4 + 4 * 256 * 256 * 2 + ts * POOL_WIDTH * 2
    return pl.pallas_call(
        functools.partial(_pool_kernel, ts=ts),
        grid=(b, nt),
        in_specs=[pl.BlockSpec((1, ts, POOL_WIDTH), lambda bi, i: (bi, i, 0)),
                  pl.BlockSpec((4, 256, 256), lambda bi, i: (0, 0, 0)),
                  pl.BlockSpec((1, POOL_WIDTH), lambda bi, i: (0, 0)),
                  pl.BlockSpec((1, POOL_WIDTH), lambda bi, i: (0, 0))],
        out_specs=pl.BlockSpec((ts, POOL_WIDTH), lambda bi, i: (bi * nt + i, 0)),
        out_shape=jax.ShapeDtypeStruct((b * s, POOL_WIDTH), BF16),
        scratch_shapes=[pltpu.VMEM((ts + POOL_HALO, POOL_WIDTH), F32)],
        compiler_params=pltpu.CompilerParams(
            dimension_semantics=("arbitrary", "arbitrary"),
            vmem_limit_bytes=_vmem_limit(blocks + (ts + POOL_HALO) * POOL_WIDTH * 4)),
        name="pool",
    )(proj3, pool_w_bf, pool_b, pool_scale)


def _split3(x):
    p1 = x.astype(BF16)
    r1 = x - p1.astype(F32)
    p2 = r1.astype(BF16)
    r2 = r1 - p2.astype(F32)
    return p1, p2, r2.astype(BF16)


def _hgrn_kernel(q_ref, z_ref, v_ref, gate_ref, lbl_ref, gn_ref, o_ref, st_ref, *, ts):
    nb = HGRN_BLOCK
    half = V7X_SUBLANES
    dk = HGRN_DK

    @pl.when(pl.program_id(1) == 0)
    def _():
        st_ref[...] = jnp.zeros(st_ref.shape, F32)

    lg = lbl_ref[...]
    mx = jnp.max(lg, axis=0, keepdims=True)
    ex = jnp.exp(lg - mx)
    lb = ex[0:1, :] / jnp.sum(ex, axis=0, keepdims=True)
    gn = gn_ref[...]

    tri = (lax.broadcasted_iota(jnp.int32, (nb, nb), 0)
           >= lax.broadcasted_iota(jnp.int32, (nb, nb), 1)).astype(BF16)
    ones = jnp.ones((dk, dk), BF16)
    rowi = lax.broadcasted_iota(jnp.int32, (half, HGRN_WIDTH), 0)
    neg_inf = jnp.float32(-jnp.inf)

    def step(n, carry):
        r0 = pl.multiple_of(n * nb, nb)
        q = q_ref[0, pl.ds(r0, nb), :]
        z = z_ref[0, pl.ds(r0, nb), :]
        v = v_ref[0, pl.ds(r0, nb), :]
        gate = gate_ref[0, pl.ds(r0, nb), :]

        f = lb + (1.0 - lb) * _sigmoid(z)
        g = jnp.log(f)
        kk = 1.0 - f
        g1, g2, g3 = _split3(g)
        b = _dot(tri, g1) + _dot(tri, g2) + _dot(tri, g3)
        b_last = b[nb - 1:nb, :]
        a = (q * jnp.exp(b)).astype(BF16)
        kx = (kk * jnp.exp(b_last - b)).astype(BF16)
        dec = jnp.exp(b_last)

        b_lo, b_hi = b[0:half, :], b[half:nb, :]
        q_lo, q_hi = q[0:half, :], q[half:nb, :]
        slabs = []
        for s in range(nb):
            bs = b[s:s + 1, :]
            ks = kk[s:s + 1, :]
            if s < half:
                d_lo = b_lo - bs
                if s > 0:
                    d_lo = jnp.where(rowi >= s, d_lo, neg_inf)
                slabs.append(q_lo * jnp.exp(d_lo) * ks)
                slabs.append(q_hi * jnp.exp(b_hi - bs) * ks)
            else:
                d_hi = b_hi - bs
                if s > half:
                    d_hi = jnp.where(rowi >= s - half, d_hi, neg_inf)
                slabs.append(q_hi * jnp.exp(d_hi) * ks)
        slabs = [p.astype(BF16) for p in slabs]

        outs = []
        for h in range(HGRN_HEADS):
            hs = slice(h * dk, (h + 1) * dk)
            st = st_ref[h]
            o_h = lax.dot_general(a[:, hs], st.astype(BF16), (((1,), (1,)), ((), ())),
                                  preferred_element_type=F32)
            p_h = jnp.concatenate([p[:, hs] for p in slabs], axis=0)
            r_h = _dot(p_h, ones)
            o_lo = o_h[0:half, :]
            o_hi = o_h[half:nb, :]
            idx = 0
            for s in range(nb):
                vs = v[s:s + 1, hs]
                if s < half:
                    o_lo = o_lo + r_h[idx * half:(idx + 1) * half, :] * vs
                    idx += 1
                o_hi = o_hi + r_h[idx * half:(idx + 1) * half, :] * vs
                idx += 1
            o_h = jnp.concatenate([o_lo, o_hi], axis=0)
            ms = jnp.mean(o_h * o_h, axis=-1, keepdims=True)
            outs.append(o_h * lax.rsqrt(ms + RMS_EPS))
            upd = _dot(v[:, hs].T.astype(BF16), kx[:, hs])
            st_ref[h] = st * dec[:, hs] + upd

        o = jnp.concatenate(outs, axis=1) * gn
        o = o * (gate * _sigmoid(gate))
        o_ref[pl.ds(r0, nb), :] = o.astype(BF16)
        return carry

    lax.fori_loop(0, ts // nb, step, 0)


def _hgrn(proj3, lb_logits, g_norm, ts=512):
    b, s, _ = proj3.shape
    nt = s // ts
    w = HGRN_WIDTH

    def col(c):
        return pl.BlockSpec((1, ts, w), lambda bi, i: (bi, i, c))

    blocks = 4 * ts * w * 4 + ts * w * 2
    return pl.pallas_call(
        functools.partial(_hgrn_kernel, ts=ts),
        grid=(b, nt),
        in_specs=[col(1), col(2), col(3), col(4),
                  pl.BlockSpec((2, w), lambda bi, i: (0, 0)),
                  pl.BlockSpec((1, w), lambda bi, i: (0, 0))],
        out_specs=pl.BlockSpec((ts, w), lambda bi, i: (bi * nt + i, 0)),
        out_shape=jax.ShapeDtypeStruct((b * s, w), BF16),
        scratch_shapes=[pltpu.VMEM((HGRN_HEADS, HGRN_DK, HGRN_DK), F32)],
        compiler_params=pltpu.CompilerParams(
            dimension_semantics=("arbitrary", "arbitrary"),
            vmem_limit_bytes=_vmem_limit(blocks)),
        name="hgrn",
    )(proj3, proj3, proj3, proj3, lb_logits, g_norm)


def _mix_kernel(a_ref, b_ref, x_ref, wt_ref, wb_ref, g_ref, bt_ref, o_ref):
    mix = _dot(a_ref[...], wt_ref[...]) + _dot(b_ref[...], wb_ref[...])
    y = ALPHA * x_ref[...] + mix
    o_ref[...] = _layer_norm(y, g_ref[...], bt_ref[...])


def _mix(a_out, b_out, x2d, w_out_bf, ln_g, ln_b, tm=512):
    t, d = x2d.shape
    hw = a_out.shape[1]
    blocks = 2 * tm * hw * 2 + 2 * tm * d * 4 + d * d * 2
    return pl.pallas_call(
        _mix_kernel,
        grid=(t // tm,),
        in_specs=[pl.BlockSpec((tm, hw), lambda i: (i, 0)),
                  pl.BlockSpec((tm, hw), lambda i: (i, 0)),
                  pl.BlockSpec((tm, d), lambda i: (i, 0)),
                  pl.BlockSpec((hw, d), lambda i: (0, 0)),
                  pl.BlockSpec((hw, d), lambda i: (1, 0)),
                  pl.BlockSpec((1, d), lambda i: (0, 0)),
                  pl.BlockSpec((1, d), lambda i: (0, 0))],
        out_specs=pl.BlockSpec((tm, d), lambda i: (i, 0)),
        out_shape=jax.ShapeDtypeStruct((t, d), F32),
        compiler_params=pltpu.CompilerParams(
            dimension_semantics=("arbitrary",),
            vmem_limit_bytes=_vmem_limit(blocks)),
        name="mix",
    )(a_out, b_out, x2d, w_out_bf, w_out_bf, ln_g, ln_b)


def _ffn_kernel(x_ref, wg_ref, wv_ref, cwg_ref, cwv_ref, cbg_ref, cbv_ref, wd_ref,
                g_ref, bt_ref, o_ref, xb_ref, ug_ref, uv_ref, tail_ref,
                *, tm, tiles_per_seq, nf):
    i = pl.program_id(0)
    j = pl.program_id(1)
    halo = CONV_HALO

    @pl.when(j == 0)
    def _():
        x = x_ref[...]
        xb_ref[...] = x.astype(BF16)
        o_ref[...] = ALPHA * x

    seq_start = (i % tiles_per_seq) == 0

    def conv(w_ref, cw_ref, cb_ref, ubuf_ref, slot):
        u = _dot(xb_ref[...], w_ref[...])

        @pl.when(seq_start)
        def _():
            ubuf_ref[0:halo, :] = jnp.zeros((halo, u.shape[1]), F32)

        @pl.when(jnp.logical_not(seq_start))
        def _():
            ubuf_ref[0:halo, :] = tail_ref[j, slot]

        ubuf_ref[halo:halo + tm, :] = u
        u1 = ubuf_ref[halo - 1:halo - 1 + tm, :]
        u2 = ubuf_ref[halo - 2:halo - 2 + tm, :]
        tail_ref[j, slot] = ubuf_ref[tm:tm + halo, :]
        cw = cw_ref[...]
        return cb_ref[...] + cw[0:1, :] * u2 + cw[1:2, :] * u1 + cw[2:3, :] * u

    gc = conv(wg_ref, cwg_ref, cbg_ref, ug_ref, 0)
    vc = conv(wv_ref, cwv_ref, cbv_ref, uv_ref, 1)
    hidden = (gc * _sigmoid(gc)) * vc
    o_ref[...] += _dot(hidden.astype(BF16), wd_ref[...])

    @pl.when(j == nf - 1)
    def _():
        o_ref[...] = _layer_norm(o_ref[...], g_ref[...], bt_ref[...])


def _ffn(x1, w_up_bf, conv_w, conv_b, w_down_bf, ln_g, ln_b, seq, tm=512, fc=512):
    t, d = x1.shape
    nf = D_FF // fc
    blocks = (2 * tm * d * 4 + 2 * d * fc * 2 + fc * d * 2 + 8 * fc * 4)
    scratch = tm * d * 2 + 2 * (tm + CONV_HALO) * fc * 4 + nf * 2 * CONV_HALO * fc * 4
    return pl.pallas_call(
        functools.partial(_ffn_kernel, tm=tm, tiles_per_seq=seq // tm, nf=nf),
        grid=(t // tm, nf),
        in_specs=[pl.BlockSpec((tm, d), lambda i, j: (i, 0)),
                  pl.BlockSpec((d, fc), lambda i, j: (0, j)),
                  pl.BlockSpec((d, fc), lambda i, j: (0, j + nf)),
                  pl.BlockSpec((CONV_WIDTH, fc), lambda i, j: (0, j)),
                  pl.BlockSpec((CONV_WIDTH, fc), lambda i, j: (0, j + nf)),
                  pl.BlockSpec((1, fc), lambda i, j: (0, j)),
                  pl.BlockSpec((1, fc), lambda i, j: (0, j + nf)),
                  pl.BlockSpec((fc, d), lambda i, j: (j, 0)),
                  pl.BlockSpec((1, d), lambda i, j: (0, 0)),
                  pl.BlockSpec((1, d), lambda i, j: (0, 0))],
        out_specs=pl.BlockSpec((tm, d), lambda i, j: (i, 0)),
        out_shape=jax.ShapeDtypeStruct((t, d), F32),
        scratch_shapes=[pltpu.VMEM((tm, d), BF16),
                        pltpu.VMEM((tm + CONV_HALO, fc), F32),
                        pltpu.VMEM((tm + CONV_HALO, fc), F32),
                        pltpu.VMEM((nf, 2, CONV_HALO, fc), F32)],
        compiler_params=pltpu.CompilerParams(
            dimension_semantics=("arbitrary", "arbitrary"),
            vmem_limit_bytes=_vmem_limit(blocks + scratch)),
        name="ffn",
    )(x1, w_up_bf, w_up_bf, conv_w, conv_w, conv_b, conv_b, w_down_bf, ln_g, ln_b)


def kernel(x, w_in, pool_w, pool_b, pool_scale, hgrn_lb_logits, hgrn_g_norm, w_out,
           ln1_g, ln1_b, w_up, conv_w, conv_b, w_down, ln2_g, ln2_b):
    bsz, seq, d = x.shape
    assert (d, w_in.shape[0], hgrn_lb_logits.shape[0]) == (D_MODEL, DEPTH, DEPTH + 1)
    t = bsz * seq
    x2d = x.reshape(t, d)

    proj = _proj(x2d, w_in[0].astype(BF16))
    proj3 = proj.reshape(bsz, seq, IN_COLS)
    a_out = _pool(proj3, pool_w[0].astype(BF16), pool_b[0].reshape(1, POOL_WIDTH),
                  pool_scale[0].reshape(1, POOL_WIDTH))
    b_out = _hgrn(proj3, hgrn_lb_logits, hgrn_g_norm[0].reshape(1, HGRN_WIDTH))
    x1 = _mix(a_out, b_out, x2d, w_out[0].astype(BF16),
              ln1_g[0].reshape(1, d), ln1_b[0].reshape(1, d))
    out = _ffn(x1, w_up[0].astype(BF16), conv_w[0], conv_b[0].reshape(1, 2 * D_FF),
               w_down[0].astype(BF16), ln2_g[0].reshape(1, d), ln2_b[0].reshape(1, d), seq)
    return out.reshape(bsz, seq, d)
```

```python
import functools

import jax
import jax.numpy as jnp
from jax import lax
from jax.experimental import pallas as pl
from jax.experimental.pallas import tpu as pltpu

D_MODEL = 2048
POOL_WIDTH = 1024
POOL_WINDOWS = (2, 4, 8, 16)
POOL_GROUP_DIM = 256
HGRN_WIDTH = 1024
HGRN_HEADS = 8
HGRN_DK = 128
IN_COLS = 5120
D_FF = 5632
CONV_WIDTH = 3
DEPTH = 1
ALPHA = (2.0 * DEPTH) ** 0.25
LN_EPS = 1e-5
RMS_EPS = 1e-6

V7X_LANES = 128
V7X_SUBLANES = 8
V7X_VMEM_BYTES = 64 * 1024 * 1024

HGRN_BLOCK = 16
POOL_HALO = 16
CONV_HALO = V7X_SUBLANES

BF16 = jnp.bfloat16
F32 = jnp.float32


def _vmem_limit(block_bytes):
    return int(min(V7X_VMEM_BYTES * 7 // 8, 2 * block_bytes + 16 * 1024 * 1024))


def _dot(a, b):
    return jnp.dot(a, b, preferred_element_type=F32)


def _sigmoid(x):
    return 1.0 / (1.0 + jnp.exp(-x))


def _layer_norm(y, g, b):
    mu = jnp.mean(y, axis=-1, keepdims=True)
    yc = y - mu
    var = jnp.mean(yc * yc, axis=-1, keepdims=True)
    return yc * lax.rsqrt(var + LN_EPS) * g + b


def _proj_kernel(x_ref, w_ref, o_ref, xb_ref):
    @pl.when(pl.program_id(1) == 0)
    def _():
        xb_ref[...] = x_ref[...].astype(BF16)

    o_ref[...] = _dot(xb_ref[...], w_ref[...])


def _proj(x2d, w_bf, tm=1024, tn=1024):
    t, d = x2d.shape
    n = w_bf.shape[1]
    blocks = tm * d * 4 + d * tn * 2 + tm * tn * 4
    return pl.pallas_call(
        _proj_kernel,
        grid=(t // tm, n // tn),
        in_specs=[pl.BlockSpec((tm, d), lambda i, j: (i, 0)),
                  pl.BlockSpec((d, tn), lambda i, j: (0, j))],
        out_specs=pl.BlockSpec((tm, tn), lambda i, j: (i, j)),
        out_shape=jax.ShapeDtypeStruct((t, n), F32),
        scratch_shapes=[pltpu.VMEM((tm, d), BF16)],
        compiler_params=pltpu.CompilerParams(
            dimension_semantics=("arbitrary", "arbitrary"),
            vmem_limit_bytes=_vmem_limit(blocks + tm * d * 2)),
        name="proj",
    )(x2d, w_bf)


def _pool_kernel(u_ref, w_ref, pb_ref, ps_ref, o_ref, buf_ref, *, ts):
    i = pl.program_id(1)

    @pl.when(i == 0)
    def _():
        buf_ref[0:POOL_HALO, :] = jnp.zeros((POOL_HALO, POOL_WIDTH), F32)

    @pl.when(i > 0)
    def _():
        buf_ref[0:POOL_HALO, :] = buf_ref[ts:ts + POOL_HALO, :]

    buf_ref[POOL_HALO:POOL_HALO + ts, :] = u_ref[0]

    pos = i * ts + lax.broadcasted_iota(jnp.int32, (ts, POOL_GROUP_DIM), 0)
    for gi, w in enumerate(POOL_WINDOWS):
        cols = slice(gi * POOL_GROUP_DIM, (gi + 1) * POOL_GROUP_DIM)
        cur = buf_ref[POOL_HALO:POOL_HALO + ts, cols]
        win = cur
        for j in range(1, w):
            win = win + buf_ref[POOL_HALO - j:POOL_HALO - j + ts, cols]
        cnt = jnp.minimum(pos + 1, w).astype(F32)
        pooled = win / cnt - cur
        y = _dot(pooled.astype(BF16), w_ref[gi]) + pb_ref[:, cols]
        o_ref[:, cols] = (y * ps_ref[:, cols]).astype(BF16)


def _pool(proj3, pool_w_bf, pool_b, pool_scale, ts=512):
    b, s, _ = proj3.shape
    nt = s // ts
    blocks = ts * POOL_WIDTH * 4 + 4 * 256 * 256 * 2 + ts * POOL_WIDTH * 2
    return pl.pallas_call(
        functools.partial(_pool_kernel, ts=ts),
        grid=(b, nt),
        in_specs=[pl.BlockSpec((1, ts, POOL_WIDTH), lambda bi, i: (bi, i, 0)),
                  pl.BlockSpec((4, 256, 256), lambda bi, i: (0, 0, 0)),
                  pl.BlockSpec((1, POOL_WIDTH), lambda bi, i: (0, 0)),
                  pl.BlockSpec((1, POOL_WIDTH), lambda bi, i: (0, 0))],
        out_specs=pl.BlockSpec((ts, POOL_WIDTH), lambda bi, i: (bi * nt + i, 0)),
        out_shape=jax.ShapeDtypeStruct((b * s, POOL_WIDTH), BF16),
        scratch_shapes=[pltpu.VMEM((ts + POOL_HALO, POOL_WIDTH), F32)],
        compiler_params=pltpu.CompilerParams(
            dimension_semantics=("arbitrary", "arbitrary"),
            vmem_limit_bytes=_vmem_limit(blocks + (ts + POOL_HALO) * POOL_WIDTH * 4)),
        name="pool",
    )(proj3, pool_w_bf, pool_b, pool_scale)


def _split3(x):
    p1 = x.astype(BF16)
    r1 = x - p1.astype(F32)
    p2 = r1.astype(BF16)
    r2 = r1 - p2.astype(F32)
    return p1, p2, r2.astype(BF16)


def _hgrn_kernel(q_ref, z_ref, v_ref, gate_ref, lbl_ref, gn_ref, o_ref, st_ref, *, ts):
    nb = HGRN_BLOCK
    half = V7X_SUBLANES
    dk = HGRN_DK

    @pl.when(pl.program_id(1) == 0)
    def _():
        st_ref[...] = jnp.zeros(st_ref.shape, F32)

    lg = lbl_ref[...]
    mx = jnp.max(lg, axis=0, keepdims=True)
    ex = jnp.exp(lg - mx)
    lb = ex[0:1, :] / jnp.sum(ex, axis=0, keepdims=True)
    gn = gn_ref[...]

    tri = (lax.broadcasted_iota(jnp.int32, (nb, nb), 0)
           >= lax.broadcasted_iota(jnp.int32, (nb, nb), 1)).astype(BF16)
    ones = jnp.ones((dk, dk), BF16)
    rowi = lax.broadcasted_iota(jnp.int32, (half, HGRN_WIDTH), 0)
    neg_inf = jnp.float32(-jnp.inf)

    def step(n, carry):
        r0 = pl.multiple_of(n * nb, nb)
        q = q_ref[0, pl.ds(r0, nb), :]
        z = z_ref[0, pl.ds(r0, nb), :]
        v = v_ref[0, pl.ds(r0, nb), :]
        gate = gate_ref[0, pl.ds(r0, nb), :]

        f = lb + (1.0 - lb) * _sigmoid(z)
        g = jnp.log(f)
        kk = 1.0 - f
        g1, g2, g3 = _split3(g)
        b = _dot(tri, g1) + _dot(tri, g2) + _dot(tri, g3)
        b_last = b[nb - 1:nb, :]
        a = (q * jnp.exp(b)).astype(BF16)
        kx = (kk * jnp.exp(b_last - b)).astype(BF16)
        dec = jnp.exp(b_last)

        b_lo, b_hi = b[0:half, :], b[half:nb, :]
        q_lo, q_hi = q[0:half, :], q[half:nb, :]
        slabs = []
        for s in range(nb):
            bs = b[s:s + 1, :]
            ks = kk[s:s + 1, :]
            if s < half:
                d_lo = b_lo - bs
                if s > 0:
                    d_lo = jnp.where(rowi >= s, d_lo, neg_inf)
                slabs.append(q_lo * jnp.exp(d_lo) * ks)
                slabs.append(q_hi * jnp.exp(b_hi - bs) * ks)
            else:
                d_hi = b_hi - bs
                if s > half:
                    d_hi = jnp.where(rowi >= s - half, d_hi, neg_inf)
                slabs.append(q_hi * jnp.exp(d_hi) * ks)
        slabs = [p.astype(BF16) for p in slabs]

        outs = []
        for h in range(HGRN_HEADS):
            hs = slice(h * dk, (h + 1) * dk)
            st = st_ref[h]
            o_h = lax.dot_general(a[:, hs], st.astype(BF16), (((1,), (1,)), ((), ())),
                                  preferred_element_type=F32)
            p_h = jnp.concatenate([p[:, hs] for p in slabs], axis=0)
            r_h = _dot(p_h, ones)
            o_lo = o_h[0:half, :]
            o_hi = o_h[half:nb, :]
            idx = 0
            for s in range(nb):
                vs = v[s:s + 1, hs]
                if s < half:
                    o_lo = o_lo + r_h[idx * half:(idx + 1) * half, :] * vs
                    idx += 1
                o_hi = o_hi + r_h[idx * half:(idx + 1) * half, :] * vs
                idx += 1
            o_h = jnp.concatenate([o_lo, o_hi], axis=0)
            ms = jnp.mean(o_h * o_h, axis=-1, keepdims=True)
            outs.append(o_h * lax.rsqrt(ms + RMS_EPS))
            upd = _dot(v[:, hs].T.astype(BF16), kx[:, hs])
            st_ref[h] = st * dec[:, hs] + upd

        o = jnp.concatenate(outs, axis=1) * gn
        o = o * (gate * _sigmoid(gate))
        o_ref[pl.ds(r0, nb), :] = o.astype(BF16)
        return carry

    lax.fori_loop(0, ts // nb, step, 0)


def _hgrn(proj3, lb_logits, g_norm, ts=512):
    b, s, _ = proj3.shape
    nt = s // ts
    w = HGRN_WIDTH

    def col(c):
        return pl.BlockSpec((1, ts, w), lambda bi, i: (bi, i, c))

    blocks = 4 * ts * w * 4 + ts * w * 2
    return pl.pallas_call(
        functools.partial(_hgrn_kernel, ts=ts),
        grid=(b, nt),
        in_specs=[col(1), col(2), col(3), col(4),
                  pl.BlockSpec((2, w), lambda bi, i: (0, 0)),
                  pl.BlockSpec((1, w), lambda bi, i: (0, 0))],
        out_specs=pl.BlockSpec((ts, w), lambda bi, i: (bi * nt + i, 0)),
        out_shape=jax.ShapeDtypeStruct((b * s, w), BF16),
        scratch_shapes=[pltpu.VMEM((HGRN_HEADS, HGRN_DK, HGRN_DK), F32)],
        compiler_params=pltpu.CompilerParams(
            dimension_semantics=("arbitrary", "arbitrary"),
            vmem_limit_bytes=_vmem_limit(blocks)),
        name="hgrn",
    )(proj3, proj3, proj3, proj3, lb_logits, g_norm)


def _mix_kernel(a_ref, b_ref, x_ref, wt_ref, wb_ref, g_ref, bt_ref, o_ref):
    mix = _dot(a_ref[...], wt_ref[...]) + _dot(b_ref[...], wb_ref[...])
    y = ALPHA * x_ref[...] + mix
    o_ref[...] = _layer_norm(y, g_ref[...], bt_ref[...])


def _mix(a_out, b_out, x2d, w_out_bf, ln_g, ln_b, tm=512):
    t, d = x2d.shape
    hw = a_out.shape[1]
    blocks = 2 * tm * hw * 2 + 2 * tm * d * 4 + d * d * 2
    return pl.pallas_call(
        _mix_kernel,
        grid=(t // tm,),
        in_specs=[pl.BlockSpec((tm, hw), lambda i: (i, 0)),
                  pl.BlockSpec((tm, hw), lambda i: (i, 0)),
                  pl.BlockSpec((tm, d), lambda i: (i, 0)),
                  pl.BlockSpec((hw, d), lambda i: (0, 0)),
                  pl.BlockSpec((hw, d), lambda i: (1, 0)),
                  pl.BlockSpec((1, d), lambda i: (0, 0)),
                  pl.BlockSpec((1, d), lambda i: (0, 0))],
        out_specs=pl.BlockSpec((tm, d), lambda i: (i, 0)),
        out_shape=jax.ShapeDtypeStruct((t, d), F32),
        compiler_params=pltpu.CompilerParams(
            dimension_semantics=("arbitrary",),
            vmem_limit_bytes=_vmem_limit(blocks)),
        name="mix",
    )(a_out, b_out, x2d, w_out_bf, w_out_bf, ln_g, ln_b)


def _ffn_kernel(x_ref, wg_ref, wv_ref, cwg_ref, cwv_ref, cbg_ref, cbv_ref, wd_ref,
                g_ref, bt_ref, o_ref, xb_ref, ug_ref, uv_ref, tail_ref, h_ref,
                *, tm, tiles_per_seq, nf, ns):
    s = pl.program_id(0)
    su = jnp.minimum(s, ns - 1)
    iu = su // nf
    ju = su % nf
    jd = jnp.maximum(s - 1, 0) % nf
    halo = CONV_HALO

    @pl.when(s == 0)
    def _():
        tail_ref[...] = jnp.zeros(tail_ref.shape, F32)
        h_ref[...] = jnp.zeros(h_ref.shape, BF16)

    @pl.when(ju == 0)
    def _():
        xb_ref[...] = x_ref[...].astype(BF16)

    @pl.when(jd == 0)
    def _():
        o_ref[...] = ALPHA * x_ref[...]

    seq_start = (iu % tiles_per_seq) == 0

    def conv(w_ref, cw_ref, cb_ref, ubuf_ref, slot):
        u = _dot(xb_ref[...], w_ref[...])
        ubuf_ref[0:halo, :] = jnp.where(seq_start, 0.0, tail_ref[ju, slot])
        ubuf_ref[halo:halo + tm, :] = u
        u1 = ubuf_ref[halo - 1:halo - 1 + tm, :]
        u2 = ubuf_ref[halo - 2:halo - 2 + tm, :]
        tail_ref[ju, slot] = ubuf_ref[tm:tm + halo, :]
        cw = cw_ref[...]
        return cb_ref[...] + cw[0:1, :] * u2 + cw[1:2, :] * u1 + cw[2:3, :] * u

    def body(h_prev_ref, h_next_ref):
        gc = conv(wg_ref, cwg_ref, cbg_ref, ug_ref, 0)
        vc = conv(wv_ref, cwv_ref, cbv_ref, uv_ref, 1)
        o_ref[...] += _dot(h_prev_ref[...], wd_ref[...])
        h_next_ref[...] = ((gc * _sigmoid(gc)) * vc).astype(BF16)

    @pl.when(s % 2 == 0)
    def _():
        body(h_ref.at[1], h_ref.at[0])

    @pl.when(s % 2 == 1)
    def _():
        body(h_ref.at[0], h_ref.at[1])

    @pl.when(jnp.logical_and(jd == nf - 1, s > 0))
    def _():
        o_ref[...] = _layer_norm(o_ref[...], g_ref[...], bt_ref[...])


def _ffn(x1, w_up_bf, conv_w, conv_b, w_down_bf, ln_g, ln_b, seq, tm=512, fc=512):
    t, d = x1.shape
    nf = D_FF // fc
    ns = (t // tm) * nf

    def up(s):
        return jnp.minimum(s, ns - 1)

    def down(s):
        return jnp.maximum(s - 1, 0)

    blocks = (2 * tm * d * 4 + 2 * d * fc * 2 + fc * d * 2 + 8 * fc * 4)
    scratch = (tm * d * 2 + 2 * (tm + CONV_HALO) * fc * 4 + nf * 2 * CONV_HALO * fc * 4
               + 2 * tm * fc * 2)
    return pl.pallas_call(
        functools.partial(_ffn_kernel, tm=tm, tiles_per_seq=seq // tm, nf=nf, ns=ns),
        grid=(ns + 1,),
        in_specs=[pl.BlockSpec((tm, d), lambda s: (up(s) // nf, 0)),
                  pl.BlockSpec((d, fc), lambda s: (0, up(s) % nf)),
                  pl.BlockSpec((d, fc), lambda s: (0, up(s) % nf + nf)),
                  pl.BlockSpec((CONV_WIDTH, fc), lambda s: (0, up(s) % nf)),
                  pl.BlockSpec((CONV_WIDTH, fc), lambda s: (0, up(s) % nf + nf)),
                  pl.BlockSpec((1, fc), lambda s: (0, up(s) % nf)),
                  pl.BlockSpec((1, fc), lambda s: (0, up(s) % nf + nf)),
                  pl.BlockSpec((fc, d), lambda s: (down(s) % nf, 0)),
                  pl.BlockSpec((1, d), lambda s: (0, 0)),
                  pl.BlockSpec((1, d), lambda s: (0, 0))],
        out_specs=pl.BlockSpec((tm, d), lambda s: (down(s) // nf, 0)),
        out_shape=jax.ShapeDtypeStruct((t, d), F32),
        scratch_shapes=[pltpu.VMEM((tm, d), BF16),
                        pltpu.VMEM((tm + CONV_HALO, fc), F32),
                        pltpu.VMEM((tm + CONV_HALO, fc), F32),
                        pltpu.VMEM((nf, 2, CONV_HALO, fc), F32),
                        pltpu.VMEM((2, tm, fc), BF16)],
        compiler_params=pltpu.CompilerParams(
            dimension_semantics=("arbitrary",),
            vmem_limit_bytes=_vmem_limit(blocks + scratch)),
        name="ffn",
    )(x1, w_up_bf, w_up_bf, conv_w, conv_w, conv_b, conv_b, w_down_bf, ln_g, ln_b)


def kernel(x, w_in, pool_w, pool_b, pool_scale, hgrn_lb_logits, hgrn_g_norm, w_out,
           ln1_g, ln1_b, w_up, conv_w, conv_b, w_down, ln2_g, ln2_b):
    bsz, seq, d = x.shape
    assert (d, w_in.shape[0], hgrn_lb_logits.shape[0]) == (D_MODEL, DEPTH, DEPTH + 1)
    t = bsz * seq
    x2d = x.reshape(t, d)

    proj = _proj(x2d, w_in[0].astype(BF16))
    proj3 = proj.reshape(bsz, seq, IN_COLS)
    a_out = _pool(proj3, pool_w[0].astype(BF16), pool_b[0].reshape(1, POOL_WIDTH),
                  pool_scale[0].reshape(1, POOL_WIDTH))
    b_out = _hgrn(proj3, hgrn_lb_logits, hgrn_g_norm[0].reshape(1, HGRN_WIDTH))
    x1 = _mix(a_out, b_out, x2d, w_out[0].astype(BF16),
              ln1_g[0].reshape(1, d), ln1_b[0].reshape(1, d))
    out = _ffn(x1, w_up[0].astype(BF16), conv_w[0], conv_b[0].reshape(1, 2 * D_FF),
               w_down[0].astype(BF16), ln2_g[0].reshape(1, d), ln2_b[0].reshape(1, d), seq)
    return out.reshape(bsz, seq, d)
```

```python
import functools

import jax
import jax.numpy as jnp
from jax import lax
from jax.experimental import pallas as pl
from jax.experimental.pallas import tpu as pltpu

D_MODEL = 2048
POOL_WIDTH = 1024
POOL_WINDOWS = (2, 4, 8, 16)
POOL_GROUP_DIM = 256
HGRN_WIDTH = 1024
HGRN_HEADS = 8
HGRN_DK = 128
IN_COLS = 5120
D_FF = 5632
CONV_WIDTH = 3
DEPTH = 1
ALPHA = (2.0 * DEPTH) ** 0.25
LN_EPS = 1e-5
RMS_EPS = 1e-6

V7X_LANES = 128
V7X_SUBLANES = 8
V7X_VMEM_BYTES = 64 * 1024 * 1024

HGRN_BLOCK = 16
POOL_HALO = 16
CONV_HALO = V7X_SUBLANES

BF16 = jnp.bfloat16
F32 = jnp.float32


def _vmem_limit(block_bytes):
    return int(min(V7X_VMEM_BYTES * 7 // 8, 2 * block_bytes + 16 * 1024 * 1024))


def _dot(a, b):
    return jnp.dot(a, b, preferred_element_type=F32)


def _sigmoid(x):
    return 1.0 / (1.0 + jnp.exp(-x))


def _layer_norm(y, g, b):
    mu = jnp.mean(y, axis=-1, keepdims=True)
    yc = y - mu
    var = jnp.mean(yc * yc, axis=-1, keepdims=True)
    return yc * lax.rsqrt(var + LN_EPS) * g + b


def _proj_kernel(x_ref, w_ref, o_ref, xb_ref):
    @pl.when(pl.program_id(1) == 0)
    def _():
        xb_ref[...] = x_ref[...].astype(BF16)

    o_ref[...] = _dot(xb_ref[...], w_ref[...])


def _proj(x2d, w_bf, tm=1024, tn=1024):
    t, d = x2d.shape
    n = w_bf.shape[1]
    blocks = tm * d * 4 + d * tn * 2 + tm * tn * 4
    return pl.pallas_call(
        _proj_kernel,
        grid=(t // tm, n // tn),
        in_specs=[pl.BlockSpec((tm, d), lambda i, j: (i, 0)),
                  pl.BlockSpec((d, tn), lambda i, j: (0, j))],
        out_specs=pl.BlockSpec((tm, tn), lambda i, j: (i, j)),
        out_shape=jax.ShapeDtypeStruct((t, n), F32),
        scratch_shapes=[pltpu.VMEM((tm, d), BF16)],
        compiler_params=pltpu.CompilerParams(
            dimension_semantics=("arbitrary", "arbitrary"),
            vmem_limit_bytes=_vmem_limit(blocks + tm * d * 2)),
        name="proj",
    )(x2d, w_bf)


def _pool_kernel(u_ref, w_ref, pb_ref, ps_ref, o_ref, buf_ref, *, ts):
    i = pl.program_id(1)

    @pl.when(i == 0)
    def _():
        buf_ref[0:POOL_HALO, :] = jnp.zeros((POOL_HALO, POOL_WIDTH), F32)

    @pl.when(i > 0)
    def _():
        buf_ref[0:POOL_HALO, :] = buf_ref[ts:ts + POOL_HALO, :]

    buf_ref[POOL_HALO:POOL_HALO + ts, :] = u_ref[0]

    pos = i * ts + lax.broadcasted_iota(jnp.int32, (ts, POOL_GROUP_DIM), 0)
    for gi, w in enumerate(POOL_WINDOWS):
        cols = slice(gi * POOL_GROUP_DIM, (gi + 1) * POOL_GROUP_DIM)
        cur = buf_ref[POOL_HALO:POOL_HALO + ts, cols]
        win = cur
        for j in range(1, w):
            win = win + buf_ref[POOL_HALO - j:POOL_HALO - j + ts, cols]
        cnt = jnp.minimum(pos + 1, w).astype(F32)
        pooled = win / cnt - cur
        y = _dot(pooled.astype(BF16), w_ref[gi]) + pb_ref[:, cols]
        o_ref[:, cols] = (y * ps_ref[:, cols]).astype(BF16)


def _pool(proj3, pool_w_bf, pool_b, pool_scale, ts=512):
    b, s, _ = proj3.shape
    nt = s // ts
    blocks = ts * POOL_WIDTH * 4 + 4 * 256 * 256 * 2 + ts * POOL_WIDTH * 2
    return pl.pallas_call(
        functools.partial(_pool_kernel, ts=ts),
        grid=(b, nt),
        in_specs=[pl.BlockSpec((1, ts, POOL_WIDTH), lambda bi, i: (bi, i, 0)),
                  pl.BlockSpec((4, 256, 256), lambda bi, i: (0, 0, 0)),
                  pl.BlockSpec((1, POOL_WIDTH), lambda bi, i: (0, 0)),
                  pl.BlockSpec((1, POOL_WIDTH), lambda bi, i: (0, 0))],
        out_specs=pl.BlockSpec((ts, POOL_WIDTH), lambda bi, i: (bi * nt + i, 0)),
        out_shape=jax.ShapeDtypeStruct((b * s, POOL_WIDTH), BF16),
        scratch_shapes=[pltpu.VMEM((ts + POOL_HALO, POOL_WIDTH), F32)],
        compiler_params=pltpu.CompilerParams(
            dimension_semantics=("arbitrary", "arbitrary"),
            vmem_limit_bytes=_vmem_limit(blocks + (ts + POOL_HALO) * POOL_WIDTH * 4)),
        name="pool",
    )(proj3, pool_w_bf, pool_b, pool_scale)


def _split3(x):
    p1 = x.astype(BF16)
    r1 = x - p1.astype(F32)
    p2 = r1.astype(BF16)
    r2 = r1 - p2.astype(F32)
    return p1, p2, r2.astype(BF16)


def _hgrn_kernel(q_ref, z_ref, v_ref, gate_ref, lbl_ref, gn_ref, o_ref, st_ref, *, ts):
    nb = HGRN_BLOCK
    half = V7X_SUBLANES
    dk = HGRN_DK
    nh = HGRN_HEADS
    width = HGRN_WIDTH

    @pl.when(pl.program_id(0) == 0)
    def _():
        st_ref[...] = jnp.zeros(st_ref.shape, F32)

    lg = lbl_ref[...]
    mx = jnp.max(lg, axis=0, keepdims=True)
    ex = jnp.exp(lg - mx)
    lb = ex[0:1, :] / jnp.sum(ex, axis=0, keepdims=True)
    gn = gn_ref[...]

    tri = (lax.broadcasted_iota(jnp.int32, (nb, nb), 0)
           >= lax.broadcasted_iota(jnp.int32, (nb, nb), 1)).astype(BF16)
    rowi = lax.broadcasted_iota(jnp.int32, (half, width), 0)
    sel = (lax.broadcasted_iota(jnp.int32, (nb, nb * dk), 0)
           == lax.broadcasted_iota(jnp.int32, (nb, nb * dk), 1) // dk).astype(BF16)
    lane_head = lax.broadcasted_iota(jnp.int32, (nb, nh * nb), 1) // nb
    rep = (lax.broadcasted_iota(jnp.int32, (nb * half, nb), 0) // half
           == lax.broadcasted_iota(jnp.int32, (nb * half, nb), 1)).astype(BF16)

    def front(n, bi):
        r0 = pl.multiple_of(n * nb, nb)
        q = q_ref[bi, pl.ds(r0, nb), :]
        z = z_ref[bi, pl.ds(r0, nb), :]
        q_rep = _dot(rep, q.astype(BF16))

        f = lb + (1.0 - lb) * _sigmoid(z)
        kk = 1.0 - f
        k_lo, k_hi = kk[0:half, :], kk[half:nb, :]
        g1, g2, g3 = _split3(jnp.log(f))
        b = _dot(tri, g1) + _dot(tri, g2) + _dot(tri, g3)
        a = (q * jnp.exp(b)).astype(BF16)
        dec = jnp.exp(b[nb - 1:nb, :])

        u_lo = None
        u_hi = None
        cols = []
        for t in range(nb):
            ft = jnp.broadcast_to(f[t:t + 1, :], (half, width))
            qt = q_rep[t * half:(t + 1) * half, :]
            if t == 0:
                u_lo = jnp.where(rowi == 0, 1.0, 0.0)
            elif t < half:
                u_lo = jnp.where(rowi == t, 1.0, u_lo * ft)
            else:
                u_lo = u_lo * ft
                u_hi = jnp.where(rowi == t - half, 1.0, u_hi * ft if t > half else 0.0)
            w_lo = u_lo * k_lo
            w_hi = u_hi * k_hi if t >= half else jnp.zeros((half, width), F32)
            cols.append(jnp.concatenate([w_lo * qt, w_hi * qt], axis=0).astype(BF16))
        kx = jnp.concatenate([w_lo, w_hi], axis=0).astype(BF16)

        pcat = jnp.concatenate(
            [jnp.concatenate([c[:, h * dk:(h + 1) * dk] for c in cols], axis=1)
             for h in range(nh)], axis=0)
        sc = lax.dot_general(sel, pcat, (((1,), (1,)), ((), ())),
                             preferred_element_type=F32)
        return sc, a, kx, dec

    def back(n, bi, sc, a, kx, dec):
        r0 = pl.multiple_of(n * nb, nb)
        v = v_ref[bi, pl.ds(r0, nb), :]
        gate = gate_ref[bi, pl.ds(r0, nb), :]
        sc_heads = jnp.concatenate(
            [jnp.where(lane_head == h, sc, 0.0) for h in range(nh)], axis=0).astype(BF16)
        v_rows = jnp.concatenate([v[:, h * dk:(h + 1) * dk] for h in range(nh)],
                                 axis=0).astype(BF16)
        o_intra = _dot(sc_heads, v_rows)

        outs = []
        for h in range(nh):
            hs = slice(h * dk, (h + 1) * dk)
            st = st_ref[bi, h]
            o_h = lax.dot_general(a[:, hs], st.astype(BF16), (((1,), (1,)), ((), ())),
                                  preferred_element_type=F32)
            o_h = o_h + o_intra[h * nb:(h + 1) * nb, :]
            ms = jnp.mean(o_h * o_h, axis=-1, keepdims=True)
            outs.append(o_h * lax.rsqrt(ms + RMS_EPS))
            upd = _dot(v[:, hs].T.astype(BF16), kx[:, hs])
            st_ref[bi, h] = st * dec[:, hs] + upd

        o = jnp.concatenate(outs, axis=1) * gn
        o = o * (gate * _sigmoid(gate))
        o_ref[bi, pl.ds(r0, nb), :] = o.astype(BF16)

    seqs = range(q_ref.shape[0])

    def step(n, carry):
        nxt = tuple(front(n + 1, bi) for bi in seqs)
        for bi in seqs:
            back(n, bi, *carry[bi])
        return nxt

    last = lax.fori_loop(0, ts // nb - 1, step, tuple(front(0, bi) for bi in seqs))
    for bi in seqs:
        back(ts // nb - 1, bi, *last[bi])


def _hgrn(proj3, lb_logits, g_norm, ts=512):
    b, s, _ = proj3.shape
    nt = s // ts
    w = HGRN_WIDTH

    def col(c):
        return pl.BlockSpec((b, ts, w), lambda i: (0, i, c))

    blocks = b * (4 * ts * w * 4 + ts * w * 2)
    out = pl.pallas_call(
        functools.partial(_hgrn_kernel, ts=ts),
        grid=(nt,),
        in_specs=[col(1), col(2), col(3), col(4),
                  pl.BlockSpec((2, w), lambda i: (0, 0)),
                  pl.BlockSpec((1, w), lambda i: (0, 0))],
        out_specs=pl.BlockSpec((b, ts, w), lambda i: (0, i, 0)),
        out_shape=jax.ShapeDtypeStruct((b, s, w), BF16),
        scratch_shapes=[pltpu.VMEM((b, HGRN_HEADS, HGRN_DK, HGRN_DK), F32)],
        compiler_params=pltpu.CompilerParams(
            dimension_semantics=("arbitrary",),
            vmem_limit_bytes=_vmem_limit(blocks)),
        name="hgrn",
    )(proj3, proj3, proj3, proj3, lb_logits, g_norm)
    return out.reshape(b * s, w)


def _mix_kernel(a_ref, b_ref, x_ref, wt_ref, wb_ref, g_ref, bt_ref, o_ref):
    mix = _dot(a_ref[...], wt_ref[...]) + _dot(b_ref[...], wb_ref[...])
    y = ALPHA * x_ref[...] + mix
    o_ref[...] = _layer_norm(y, g_ref[...], bt_ref[...])


def _mix(a_out, b_out, x2d, w_out_bf, ln_g, ln_b, tm=512):
    t, d = x2d.shape
    hw = a_out.shape[1]
    blocks = 2 * tm * hw * 2 + 2 * tm * d * 4 + d * d * 2
    return pl.pallas_call(
        _mix_kernel,
        grid=(t // tm,),
        in_specs=[pl.BlockSpec((tm, hw), lambda i: (i, 0)),
                  pl.BlockSpec((tm, hw), lambda i: (i, 0)),
                  pl.BlockSpec((tm, d), lambda i: (i, 0)),
                  pl.BlockSpec((hw, d), lambda i: (0, 0)),
                  pl.BlockSpec((hw, d), lambda i: (1, 0)),
                  pl.BlockSpec((1, d), lambda i: (0, 0)),
                  pl.BlockSpec((1, d), lambda i: (0, 0))],
        out_specs=pl.BlockSpec((tm, d), lambda i: (i, 0)),
        out_shape=jax.ShapeDtypeStruct((t, d), F32),
        compiler_params=pltpu.CompilerParams(
            dimension_semantics=("arbitrary",),
            vmem_limit_bytes=_vmem_limit(blocks)),
        name="mix",
    )(a_out, b_out, x2d, w_out_bf, w_out_bf, ln_g, ln_b)


def _ffn_kernel(x_ref, wg_ref, wv_ref, cwg_ref, cwv_ref, cbg_ref, cbv_ref, wd_ref,
                g_ref, bt_ref, o_ref, xb_ref, ug_ref, uv_ref, tail_ref, h_ref,
                *, tm, tiles_per_seq, nf, ns):
    s = pl.program_id(0)
    su = jnp.minimum(s, ns - 1)
    iu = su // nf
    ju = su % nf
    jd = jnp.maximum(s - 1, 0) % nf
    halo = CONV_HALO

    @pl.when(s == 0)
    def _():
        tail_ref[...] = jnp.zeros(tail_ref.shape, F32)
        h_ref[...] = jnp.zeros(h_ref.shape, BF16)

    @pl.when(ju == 0)
    def _():
        xb_ref[...] = x_ref[...].astype(BF16)

    @pl.when(jd == 0)
    def _():
        o_ref[...] = ALPHA * x_ref[...]

    seq_start = (iu % tiles_per_seq) == 0

    def conv(w_ref, cw_ref, cb_ref, ubuf_ref, slot):
        u = _dot(xb_ref[...], w_ref[...])
        ubuf_ref[0:halo, :] = jnp.where(seq_start, 0.0, tail_ref[ju, slot])
        ubuf_ref[halo:halo + tm, :] = u
        u1 = ubuf_ref[halo - 1:halo - 1 + tm, :]
        u2 = ubuf_ref[halo - 2:halo - 2 + tm, :]
        tail_ref[ju, slot] = ubuf_ref[tm:tm + halo, :]
        cw = cw_ref[...]
        return cb_ref[...] + cw[0:1, :] * u2 + cw[1:2, :] * u1 + cw[2:3, :] * u

    def body(h_prev_ref, h_next_ref):
        gc = conv(wg_ref, cwg_ref, cbg_ref, ug_ref, 0)
        vc = conv(wv_ref, cwv_ref, cbv_ref, uv_ref, 1)
        o_ref[...] += _dot(h_prev_ref[...], wd_ref[...])
        h_next_ref[...] = ((gc * _sigmoid(gc)) * vc).astype(BF16)

    @pl.when(s % 2 == 0)
    def _():
        body(h_ref.at[1], h_ref.at[0])

    @pl.when(s % 2 == 1)
    def _():
        body(h_ref.at[0], h_ref.at[1])

    @pl.when(jnp.logical_and(jd == nf - 1, s > 0))
    def _():
        o_ref[...] = _layer_norm(o_ref[...], g_ref[...], bt_ref[...])


def _ffn(x1, w_up_bf, conv_w, conv_b, w_down_bf, ln_g, ln_b, seq, tm=512, fc=512):
    t, d = x1.shape
    nf = D_FF // fc
    ns = (t // tm) * nf

    def up(s):
        return jnp.minimum(s, ns - 1)

    def down(s):
        return jnp.maximum(s - 1, 0)

    blocks = (2 * tm * d * 4 + 2 * d * fc * 2 + fc * d * 2 + 8 * fc * 4)
    scratch = (tm * d * 2 + 2 * (tm + CONV_HALO) * fc * 4 + nf * 2 * CONV_HALO * fc * 4
               + 2 * tm * fc * 2)
    return pl.pallas_call(
        functools.partial(_ffn_kernel, tm=tm, tiles_per_seq=seq // tm, nf=nf, ns=ns),
        grid=(ns + 1,),
        in_specs=[pl.BlockSpec((tm, d), lambda s: (up(s) // nf, 0)),
                  pl.BlockSpec((d, fc), lambda s: (0, up(s) % nf)),
                  pl.BlockSpec((d, fc), lambda s: (0, up(s) % nf + nf)),
                  pl.BlockSpec((CONV_WIDTH, fc), lambda s: (0, up(s) % nf)),
                  pl.BlockSpec((CONV_WIDTH, fc), lambda s: (0, up(s) % nf + nf)),
                  pl.BlockSpec((1, fc), lambda s: (0, up(s) % nf)),
                  pl.BlockSpec((1, fc), lambda s: (0, up(s) % nf + nf)),
                  pl.BlockSpec((fc, d), lambda s: (down(s) % nf, 0)),
                  pl.BlockSpec((1, d), lambda s: (0, 0)),
                  pl.BlockSpec((1, d), lambda s: (0, 0))],
        out_specs=pl.BlockSpec((tm, d), lambda s: (down(s) // nf, 0)),
        out_shape=jax.ShapeDtypeStruct((t, d), F32),
        scratch_shapes=[pltpu.VMEM((tm, d), BF16),
                        pltpu.VMEM((tm + CONV_HALO, fc), F32),
                        pltpu.VMEM((tm + CONV_HALO, fc), F32),
                        pltpu.VMEM((nf, 2, CONV_HALO, fc), F32),
                        pltpu.VMEM((2, tm, fc), BF16)],
        compiler_params=pltpu.CompilerParams(
            dimension_semantics=("arbitrary",),
            vmem_limit_bytes=_vmem_limit(blocks + scratch)),
        name="ffn",
    )(x1, w_up_bf, w_up_bf, conv_w, conv_w, conv_b, conv_b, w_down_bf, ln_g, ln_b)


def kernel(x, w_in, pool_w, pool_b, pool_scale, hgrn_lb_logits, hgrn_g_norm, w_out,
           ln1_g, ln1_b, w_up, conv_w, conv_b, w_down, ln2_g, ln2_b):
    bsz, seq, d = x.shape
    assert (d, w_in.shape[0], hgrn_lb_logits.shape[0]) == (D_MODEL, DEPTH, DEPTH + 1)
    t = bsz * seq
    x2d = x.reshape(t, d)

    proj = _proj(x2d, w_in[0].astype(BF16))
    proj3 = proj.reshape(bsz, seq, IN_COLS)
    a_out = _pool(proj3, pool_w[0].astype(BF16), pool_b[0].reshape(1, POOL_WIDTH),
                  pool_scale[0].reshape(1, POOL_WIDTH))
    b_out = _hgrn(proj3, hgrn_lb_logits, hgrn_g_norm[0].reshape(1, HGRN_WIDTH))
    x1 = _mix(a_out, b_out, x2d, w_out[0].astype(BF16),
              ln1_g[0].reshape(1, d), ln1_b[0].reshape(1, d))
    out = _ffn(x1, w_up[0].astype(BF16), conv_w[0], conv_b[0].reshape(1, 2 * D_FF),
               w_down[0].astype(BF16), ln2_g[0].reshape(1, d), ln2_b[0].reshape(1, d), seq)
    return out.reshape(bsz, seq, d)
```

```python
import functools

import jax
import jax.numpy as jnp
from jax import lax
from jax.experimental import pallas as pl
from jax.experimental.pallas import tpu as pltpu

D_MODEL = 2048
POOL_WIDTH = 1024
POOL_WINDOWS = (2, 4, 8, 16)
POOL_GROUP_DIM = 256
HGRN_WIDTH = 1024
HGRN_HEADS = 8
HGRN_DK = 128
IN_COLS = 5120
D_FF = 5632
CONV_WIDTH = 3
DEPTH = 1
ALPHA = (2.0 * DEPTH) ** 0.25
LN_EPS = 1e-5
RMS_EPS = 1e-6

V7X_LANES = 128
V7X_SUBLANES = 8
V7X_VMEM_BYTES = 64 * 1024 * 1024

HGRN_BLOCK = 16
POOL_HALO = 16
CONV_HALO = V7X_SUBLANES
FFN_RES_COLS = 256

BF16 = jnp.bfloat16
F32 = jnp.float32


def _vmem_limit(block_bytes):
    return int(min(V7X_VMEM_BYTES * 7 // 8, 2 * block_bytes + 16 * 1024 * 1024))


def _dot(a, b):
    return jnp.dot(a, b, preferred_element_type=F32)


def _sigmoid(x):
    return 1.0 / (1.0 + jnp.exp(-x))


def _layer_norm(y, g, b):
    mu = jnp.mean(y, axis=-1, keepdims=True)
    yc = y - mu
    var = jnp.mean(yc * yc, axis=-1, keepdims=True)
    return yc * lax.rsqrt(var + LN_EPS) * g + b


def _proj_kernel(x_ref, w_ref, o_ref, xb_ref):
    @pl.when(pl.program_id(1) == 0)
    def _():
        xb_ref[...] = x_ref[...].astype(BF16)

    o_ref[...] = _dot(xb_ref[...], w_ref[...])


def _proj(x2d, w_bf, tm=1024, tn=1024):
    t, d = x2d.shape
    n = w_bf.shape[1]
    blocks = tm * d * 4 + d * tn * 2 + tm * tn * 4
    return pl.pallas_call(
        _proj_kernel,
        grid=(t // tm, n // tn),
        in_specs=[pl.BlockSpec((tm, d), lambda i, j: (i, 0)),
                  pl.BlockSpec((d, tn), lambda i, j: (0, j))],
        out_specs=pl.BlockSpec((tm, tn), lambda i, j: (i, j)),
        out_shape=jax.ShapeDtypeStruct((t, n), F32),
        scratch_shapes=[pltpu.VMEM((tm, d), BF16)],
        compiler_params=pltpu.CompilerParams(
            dimension_semantics=("arbitrary", "arbitrary"),
            vmem_limit_bytes=_vmem_limit(blocks + tm * d * 2)),
        name="proj",
    )(x2d, w_bf)


def _pool_kernel(u_ref, w_ref, pb_ref, ps_ref, o_ref, buf_ref, *, ts):
    i = pl.program_id(1)

    @pl.when(i == 0)
    def _():
        buf_ref[0:POOL_HALO, :] = jnp.zeros((POOL_HALO, POOL_WIDTH), F32)

    @pl.when(i > 0)
    def _():
        buf_ref[0:POOL_HALO, :] = buf_ref[ts:ts + POOL_HALO, :]

    buf_ref[POOL_HALO:POOL_HALO + ts, :] = u_ref[0]

    pos = i * ts + lax.broadcasted_iota(jnp.int32, (ts, POOL_GROUP_DIM), 0)
    for gi, w in enumerate(POOL_WINDOWS):
        cols = slice(gi * POOL_GROUP_DIM, (gi + 1) * POOL_GROUP_DIM)
        cur = buf_ref[POOL_HALO:POOL_HALO + ts, cols]
        win = cur
        for j in range(1, w):
            win = win + buf_ref[POOL_HALO - j:POOL_HALO - j + ts, cols]
        cnt = jnp.minimum(pos + 1, w).astype(F32)
        pooled = win / cnt - cur
        y = _dot(pooled.astype(BF16), w_ref[gi]) + pb_ref[:, cols]
        o_ref[:, cols] = (y * ps_ref[:, cols]).astype(BF16)


def _pool(proj3, pool_w_bf, pool_b, pool_scale, ts=512):
    b, s, _ = proj3.shape
    nt = s // ts
    blocks = ts * POOL_WIDTH * 4 + 4 * 256 * 256 * 2 + ts * POOL_WIDTH * 2
    return pl.pallas_call(
        functools.partial(_pool_kernel, ts=ts),
        grid=(b, nt),
        in_specs=[pl.BlockSpec((1, ts, POOL_WIDTH), lambda bi, i: (bi, i, 0)),
                  pl.BlockSpec((4, 256, 256), lambda bi, i: (0, 0, 0)),
                  pl.BlockSpec((1, POOL_WIDTH), lambda bi, i: (0, 0)),
                  pl.BlockSpec((1, POOL_WIDTH), lambda bi, i: (0, 0))],
        out_specs=pl.BlockSpec((ts, POOL_WIDTH), lambda bi, i: (bi * nt + i, 0)),
        out_shape=jax.ShapeDtypeStruct((b * s, POOL_WIDTH), BF16),
        scratch_shapes=[pltpu.VMEM((ts + POOL_HALO, POOL_WIDTH), F32)],
        compiler_params=pltpu.CompilerParams(
            dimension_semantics=("arbitrary", "arbitrary"),
            vmem_limit_bytes=_vmem_limit(blocks + (ts + POOL_HALO) * POOL_WIDTH * 4)),
        name="pool",
    )(proj3, pool_w_bf, pool_b, pool_scale)


def _split3(x):
    p1 = x.astype(BF16)
    r1 = x - p1.astype(F32)
    p2 = r1.astype(BF16)
    r2 = r1 - p2.astype(F32)
    return p1, p2, r2.astype(BF16)


def _hgrn_kernel(q_ref, z_ref, v_ref, gate_ref, lbl_ref, gn_ref, o_ref, st_ref, *, ts):
    nb = HGRN_BLOCK
    half = V7X_SUBLANES
    dk = HGRN_DK
    nh = HGRN_HEADS
    width = HGRN_WIDTH

    @pl.when(pl.program_id(0) == 0)
    def _():
        st_ref[...] = jnp.zeros(st_ref.shape, F32)

    lg = lbl_ref[...]
    mx = jnp.max(lg, axis=0, keepdims=True)
    ex = jnp.exp(lg - mx)
    lb = ex[0:1, :] / jnp.sum(ex, axis=0, keepdims=True)
    gn = gn_ref[...]

    tri = (lax.broadcasted_iota(jnp.int32, (nb, nb), 0)
           >= lax.broadcasted_iota(jnp.int32, (nb, nb), 1)).astype(BF16)
    rowi = lax.broadcasted_iota(jnp.int32, (half, width), 0)
    sel = (lax.broadcasted_iota(jnp.int32, (nb, nb * dk), 0)
           == lax.broadcasted_iota(jnp.int32, (nb, nb * dk), 1) // dk).astype(BF16)
    lane_head = lax.broadcasted_iota(jnp.int32, (nb, nh * nb), 1) // nb
    rep = (lax.broadcasted_iota(jnp.int32, (nb * half, nb), 0) // half
           == lax.broadcasted_iota(jnp.int32, (nb * half, nb), 1)).astype(BF16)

    def front(n, bi):
        r0 = pl.multiple_of(n * nb, nb)
        q = q_ref[bi, pl.ds(r0, nb), :]
        z = z_ref[bi, pl.ds(r0, nb), :]
        q_rep = _dot(rep, q.astype(BF16))

        f = lb + (1.0 - lb) * _sigmoid(z)
        kk = 1.0 - f
        k_lo, k_hi = kk[0:half, :], kk[half:nb, :]
        g1, g2, g3 = _split3(jnp.log(f))
        b = _dot(tri, g1) + _dot(tri, g2) + _dot(tri, g3)
        a = (q * jnp.exp(b)).astype(BF16)
        dec = jnp.exp(b[nb - 1:nb, :])

        u_lo = None
        u_hi = None
        cols = []
        for t in range(nb):
            ft = jnp.broadcast_to(f[t:t + 1, :], (half, width))
            qt = q_rep[t * half:(t + 1) * half, :]
            if t == 0:
                u_lo = jnp.where(rowi == 0, 1.0, 0.0)
            elif t < half:
                u_lo = jnp.where(rowi == t, 1.0, u_lo * ft)
            else:
                u_lo = u_lo * ft
                u_hi = jnp.where(rowi == t - half, 1.0, u_hi * ft if t > half else 0.0)
            w_lo = u_lo * k_lo
            w_hi = u_hi * k_hi if t >= half else jnp.zeros((half, width), F32)
            cols.append(jnp.concatenate([w_lo * qt, w_hi * qt], axis=0).astype(BF16))
        kx = jnp.concatenate([w_lo, w_hi], axis=0).astype(BF16)

        pcat = jnp.concatenate(
            [jnp.concatenate([c[:, h * dk:(h + 1) * dk] for c in cols], axis=1)
             for h in range(nh)], axis=0)
        sc = lax.dot_general(sel, pcat, (((1,), (1,)), ((), ())),
                             preferred_element_type=F32)
        return sc, a, kx, dec

    def back(n, bi, sc, a, kx, dec):
        r0 = pl.multiple_of(n * nb, nb)
        v = v_ref[bi, pl.ds(r0, nb), :]
        gate = gate_ref[bi, pl.ds(r0, nb), :]
        sc_heads = jnp.concatenate(
            [jnp.where(lane_head == h, sc, 0.0) for h in range(nh)], axis=0).astype(BF16)
        v_rows = jnp.concatenate([v[:, h * dk:(h + 1) * dk] for h in range(nh)],
                                 axis=0).astype(BF16)
        o_intra = _dot(sc_heads, v_rows)

        outs = []
        for h in range(nh):
            hs = slice(h * dk, (h + 1) * dk)
            st = st_ref[bi, h]
            o_h = lax.dot_general(a[:, hs], st.astype(BF16), (((1,), (1,)), ((), ())),
                                  preferred_element_type=F32)
            o_h = o_h + o_intra[h * nb:(h + 1) * nb, :]
            ms = jnp.mean(o_h * o_h, axis=-1, keepdims=True)
            outs.append(o_h * lax.rsqrt(ms + RMS_EPS))
            upd = _dot(v[:, hs].T.astype(BF16), kx[:, hs])
            st_ref[bi, h] = st * dec[:, hs] + upd

        o = jnp.concatenate(outs, axis=1) * gn
        o = o * (gate * _sigmoid(gate))
        o_ref[bi, pl.ds(r0, nb), :] = o.astype(BF16)

    seqs = range(q_ref.shape[0])

    def step(n, carry):
        nxt = tuple(front(n + 1, bi) for bi in seqs)
        for bi in seqs:
            back(n, bi, *carry[bi])
        return nxt

    last = lax.fori_loop(0, ts // nb - 1, step, tuple(front(0, bi) for bi in seqs))
    for bi in seqs:
        back(ts // nb - 1, bi, *last[bi])


def _hgrn(proj3, lb_logits, g_norm, ts=512):
    b, s, _ = proj3.shape
    nt = s // ts
    w = HGRN_WIDTH

    def col(c):
        return pl.BlockSpec((b, ts, w), lambda i: (0, i, c))

    blocks = b * (4 * ts * w * 4 + ts * w * 2)
    out = pl.pallas_call(
        functools.partial(_hgrn_kernel, ts=ts),
        grid=(nt,),
        in_specs=[col(1), col(2), col(3), col(4),
                  pl.BlockSpec((2, w), lambda i: (0, 0)),
                  pl.BlockSpec((1, w), lambda i: (0, 0))],
        out_specs=pl.BlockSpec((b, ts, w), lambda i: (0, i, 0)),
        out_shape=jax.ShapeDtypeStruct((b, s, w), BF16),
        scratch_shapes=[pltpu.VMEM((b, HGRN_HEADS, HGRN_DK, HGRN_DK), F32)],
        compiler_params=pltpu.CompilerParams(
            dimension_semantics=("arbitrary",),
            vmem_limit_bytes=_vmem_limit(blocks)),
        name="hgrn",
    )(proj3, proj3, proj3, proj3, lb_logits, g_norm)
    return out.reshape(b * s, w)


def _mix_kernel(a_ref, b_ref, x_ref, wt_ref, wb_ref, g_ref, bt_ref, o_ref, ob_ref):
    mix = _dot(a_ref[...], wt_ref[...]) + _dot(b_ref[...], wb_ref[...])
    y = ALPHA * x_ref[...] + mix
    x1 = _layer_norm(y, g_ref[...], bt_ref[...])
    o_ref[...] = x1
    ob_ref[...] = x1.astype(BF16)


def _mix(a_out, b_out, x2d, w_out_bf, ln_g, ln_b, tm=512):
    t, d = x2d.shape
    hw = a_out.shape[1]
    blocks = 2 * tm * hw * 2 + 2 * tm * d * 4 + tm * d * 2 + d * d * 2
    return pl.pallas_call(
        _mix_kernel,
        grid=(t // tm,),
        in_specs=[pl.BlockSpec((tm, hw), lambda i: (i, 0)),
                  pl.BlockSpec((tm, hw), lambda i: (i, 0)),
                  pl.BlockSpec((tm, d), lambda i: (i, 0)),
                  pl.BlockSpec((hw, d), lambda i: (0, 0)),
                  pl.BlockSpec((hw, d), lambda i: (1, 0)),
                  pl.BlockSpec((1, d), lambda i: (0, 0)),
                  pl.BlockSpec((1, d), lambda i: (0, 0))],
        out_specs=[pl.BlockSpec((tm, d), lambda i: (i, 0)),
                   pl.BlockSpec((tm, d), lambda i: (i, 0))],
        out_shape=[jax.ShapeDtypeStruct((t, d), F32), jax.ShapeDtypeStruct((t, d), BF16)],
        compiler_params=pltpu.CompilerParams(
            dimension_semantics=("arbitrary",),
            vmem_limit_bytes=_vmem_limit(blocks)),
        name="mix",
    )(a_out, b_out, x2d, w_out_bf, w_out_bf, ln_g, ln_b)


def _ffn_kernel(xb_ref, xres_ref, wg_ref, wv_ref, cwg_ref, cwv_ref, cbg_ref, cbv_ref, wd_ref,
                g_ref, bt_ref, o_ref, ug_ref, uv_ref, tail_ref, h_ref,
                *, tm, tiles_per_seq, nf, ns):
    s = pl.program_id(0)
    su = jnp.minimum(s, ns - 1)
    iu = su // nf
    ju = su % nf
    jd = jnp.maximum(s - 1, 0) % nf
    halo = CONV_HALO

    @pl.when(s == 0)
    def _():
        tail_ref[...] = jnp.zeros(tail_ref.shape, F32)
        h_ref[...] = jnp.zeros(h_ref.shape, BF16)

    @pl.when(jd == 0)
    def _():
        o_ref[...] = jnp.zeros(o_ref.shape, F32)

    seq_start = (iu % tiles_per_seq) == 0

    def conv(w_ref, cw_ref, cb_ref, ubuf_ref, slot):
        u = _dot(xb_ref[...], w_ref[...])
        ubuf_ref[0:halo, :] = jnp.where(seq_start, 0.0, tail_ref[ju, slot])
        ubuf_ref[halo:halo + tm, :] = u
        u1 = ubuf_ref[halo - 1:halo - 1 + tm, :]
        u2 = ubuf_ref[halo - 2:halo - 2 + tm, :]
        tail_ref[ju, slot] = ubuf_ref[tm:tm + halo, :]
        cw = cw_ref[...]
        return cb_ref[...] + cw[0:1, :] * u2 + cw[1:2, :] * u1 + cw[2:3, :] * u

    def body(h_prev_ref, h_next_ref):
        gc = conv(wg_ref, cwg_ref, cbg_ref, ug_ref, 0)
        vc = conv(wv_ref, cwv_ref, cbv_ref, uv_ref, 1)
        o_ref[...] += _dot(h_prev_ref[...], wd_ref[...])
        h_next_ref[...] = ((gc * _sigmoid(gc)) * vc).astype(BF16)

    @pl.when(s % 2 == 0)
    def _():
        body(h_ref.at[1], h_ref.at[0])

    @pl.when(s % 2 == 1)
    def _():
        body(h_ref.at[0], h_ref.at[1])

    for c in range(o_ref.shape[1] // FFN_RES_COLS):
        @pl.when(jd == c)
        def _():
            cols = slice(c * FFN_RES_COLS, (c + 1) * FFN_RES_COLS)
            o_ref[:, cols] += ALPHA * xres_ref[...]

    @pl.when(jnp.logical_and(jd == nf - 1, s > 0))
    def _():
        o_ref[...] = _layer_norm(o_ref[...], g_ref[...], bt_ref[...])


def _ffn(x1, x1_bf, w_up_bf, conv_w, conv_b, w_down_bf, ln_g, ln_b, seq, tm=1024, fc=512):
    t, d = x1.shape
    nf = D_FF // fc
    ns = (t // tm) * nf
    n_res = d // FFN_RES_COLS
    assert n_res <= nf

    def up(s):
        return jnp.minimum(s, ns - 1)

    def down(s):
        return jnp.maximum(s - 1, 0)

    blocks = (tm * d * 2 + tm * FFN_RES_COLS * 4 + tm * d * 4 + 2 * d * fc * 2 + fc * d * 2
              + 8 * fc * 4)
    scratch = (2 * (tm + CONV_HALO) * fc * 4 + nf * 2 * CONV_HALO * fc * 4 + 2 * tm * fc * 2)
    return pl.pallas_call(
        functools.partial(_ffn_kernel, tm=tm, tiles_per_seq=seq // tm, nf=nf, ns=ns),
        grid=(ns + 1,),
        in_specs=[pl.BlockSpec((tm, d), lambda s: (up(s) // nf, 0)),
                  pl.BlockSpec((tm, FFN_RES_COLS),
                               lambda s: (down(s) // nf, jnp.minimum(down(s) % nf, n_res - 1))),
                  pl.BlockSpec((d, fc), lambda s: (0, up(s) % nf)),
                  pl.BlockSpec((d, fc), lambda s: (0, up(s) % nf + nf)),
                  pl.BlockSpec((CONV_WIDTH, fc), lambda s: (0, up(s) % nf)),
                  pl.BlockSpec((CONV_WIDTH, fc), lambda s: (0, up(s) % nf + nf)),
                  pl.BlockSpec((1, fc), lambda s: (0, up(s) % nf)),
                  pl.BlockSpec((1, fc), lambda s: (0, up(s) % nf + nf)),
                  pl.BlockSpec((fc, d), lambda s: (down(s) % nf, 0)),
                  pl.BlockSpec((1, d), lambda s: (0, 0)),
                  pl.BlockSpec((1, d), lambda s: (0, 0))],
        out_specs=pl.BlockSpec((tm, d), lambda s: (down(s) // nf, 0)),
        out_shape=jax.ShapeDtypeStruct((t, d), F32),
        scratch_shapes=[pltpu.VMEM((tm + CONV_HALO, fc), F32),
                        pltpu.VMEM((tm + CONV_HALO, fc), F32),
                        pltpu.VMEM((nf, 2, CONV_HALO, fc), F32),
                        pltpu.VMEM((2, tm, fc), BF16)],
        compiler_params=pltpu.CompilerParams(
            dimension_semantics=("arbitrary",),
            vmem_limit_bytes=_vmem_limit(blocks + scratch)),
        name="ffn",
    )(x1_bf, x1, w_up_bf, w_up_bf, conv_w, conv_w, conv_b, conv_b, w_down_bf, ln_g, ln_b)


def kernel(x, w_in, pool_w, pool_b, pool_scale, hgrn_lb_logits, hgrn_g_norm, w_out,
           ln1_g, ln1_b, w_up, conv_w, conv_b, w_down, ln2_g, ln2_b):
    bsz, seq, d = x.shape
    assert (d, w_in.shape[0], hgrn_lb_logits.shape[0]) == (D_MODEL, DEPTH, DEPTH + 1)
    t = bsz * seq
    x2d = x.reshape(t, d)

    proj = _proj(x2d, w_in[0].astype(BF16))
    proj3 = proj.reshape(bsz, seq, IN_COLS)
    a_out = _pool(proj3, pool_w[0].astype(BF16), pool_b[0].reshape(1, POOL_WIDTH),
                  pool_scale[0].reshape(1, POOL_WIDTH))
    b_out = _hgrn(proj3, hgrn_lb_logits, hgrn_g_norm[0].reshape(1, HGRN_WIDTH))
    x1, x1_bf = _mix(a_out, b_out, x2d, w_out[0].astype(BF16),
                     ln1_g[0].reshape(1, d), ln1_b[0].reshape(1, d))
    out = _ffn(x1, x1_bf, w_up[0].astype(BF16), conv_w[0], conv_b[0].reshape(1, 2 * D_FF),
               w_down[0].astype(BF16), ln2_g[0].reshape(1, d), ln2_b[0].reshape(1, d), seq)
    return out.reshape(bsz, seq, d)
```

```python
import functools

import jax
import jax.numpy as jnp
from jax import lax
from jax.experimental import pallas as pl
from jax.experimental.pallas import tpu as pltpu

D_MODEL = 2048
POOL_WIDTH = 1024
POOL_WINDOWS = (2, 4, 8, 16)
POOL_GROUP_DIM = 256
HGRN_WIDTH = 1024
HGRN_HEADS = 8
HGRN_DK = 128
IN_COLS = 5120
D_FF = 5632
CONV_WIDTH = 3
DEPTH = 1
ALPHA = (2.0 * DEPTH) ** 0.25
LN_EPS = 1e-5
RMS_EPS = 1e-6

V7X_LANES = 128
V7X_SUBLANES = 8
V7X_VMEM_BYTES = 64 * 1024 * 1024

HGRN_BLOCK = 16
POOL_HALO = 16
CONV_HALO = V7X_SUBLANES
FFN_RES_COLS = 256

BF16 = jnp.bfloat16
F32 = jnp.float32


def _vmem_limit(block_bytes):
    return int(min(V7X_VMEM_BYTES * 7 // 8, 2 * block_bytes + 16 * 1024 * 1024))


def _dot(a, b):
    return jnp.dot(a, b, preferred_element_type=F32)


def _sigmoid(x):
    return 1.0 / (1.0 + jnp.exp(-x))


def _layer_norm(y, g, b):
    mu = jnp.mean(y, axis=-1, keepdims=True)
    yc = y - mu
    var = jnp.mean(yc * yc, axis=-1, keepdims=True)
    return yc * lax.rsqrt(var + LN_EPS) * g + b


def _proj_kernel(x_ref, w_ref, o_ref, xb_ref):
    @pl.when(pl.program_id(1) == 0)
    def _():
        xb_ref[...] = x_ref[...].astype(BF16)

    o_ref[...] = _dot(xb_ref[...], w_ref[...])


def _proj(x2d, w_bf, tm=1024, tn=1024):
    t, d = x2d.shape
    n = w_bf.shape[1]
    blocks = tm * d * 4 + d * tn * 2 + tm * tn * 4
    return pl.pallas_call(
        _proj_kernel,
        grid=(t // tm, n // tn),
        in_specs=[pl.BlockSpec((tm, d), lambda i, j: (i, 0)),
                  pl.BlockSpec((d, tn), lambda i, j: (0, j))],
        out_specs=pl.BlockSpec((tm, tn), lambda i, j: (i, j)),
        out_shape=jax.ShapeDtypeStruct((t, n), F32),
        scratch_shapes=[pltpu.VMEM((tm, d), BF16)],
        compiler_params=pltpu.CompilerParams(
            dimension_semantics=("arbitrary", "arbitrary"),
            vmem_limit_bytes=_vmem_limit(blocks + tm * d * 2)),
        name="proj",
    )(x2d, w_bf)


def _pool_kernel(u_ref, w_ref, pb_ref, ps_ref, o_ref, buf_ref, *, ts):
    i = pl.program_id(1)

    @pl.when(i == 0)
    def _():
        buf_ref[0:POOL_HALO, :] = jnp.zeros((POOL_HALO, POOL_WIDTH), F32)

    @pl.when(i > 0)
    def _():
        buf_ref[0:POOL_HALO, :] = buf_ref[ts:ts + POOL_HALO, :]

    buf_ref[POOL_HALO:POOL_HALO + ts, :] = u_ref[0]

    pos = i * ts + lax.broadcasted_iota(jnp.int32, (ts, POOL_GROUP_DIM), 0)
    for gi, w in enumerate(POOL_WINDOWS):
        cols = slice(gi * POOL_GROUP_DIM, (gi + 1) * POOL_GROUP_DIM)
        cur = buf_ref[POOL_HALO:POOL_HALO + ts, cols]
        win = cur
        for j in range(1, w):
            win = win + buf_ref[POOL_HALO - j:POOL_HALO - j + ts, cols]
        cnt = jnp.minimum(pos + 1, w).astype(F32)
        pooled = win / cnt - cur
        y = _dot(pooled.astype(BF16), w_ref[gi]) + pb_ref[:, cols]
        o_ref[:, cols] = (y * ps_ref[:, cols]).astype(BF16)


def _pool(proj3, pool_w_bf, pool_b, pool_scale, ts=512):
    b, s, _ = proj3.shape
    nt = s // ts
    blocks = ts * POOL_WIDTH * 4 + 4 * 256 * 256 * 2 + ts * POOL_WIDTH * 2
    return pl.pallas_call(
        functools.partial(_pool_kernel, ts=ts),
        grid=(b, nt),
        in_specs=[pl.BlockSpec((1, ts, POOL_WIDTH), lambda bi, i: (bi, i, 0)),
                  pl.BlockSpec((4, 256, 256), lambda bi, i: (0, 0, 0)),
                  pl.BlockSpec((1, POOL_WIDTH), lambda bi, i: (0, 0)),
                  pl.BlockSpec((1, POOL_WIDTH), lambda bi, i: (0, 0))],
        out_specs=pl.BlockSpec((ts, POOL_WIDTH), lambda bi, i: (bi * nt + i, 0)),
        out_shape=jax.ShapeDtypeStruct((b * s, POOL_WIDTH), BF16),
        scratch_shapes=[pltpu.VMEM((ts + POOL_HALO, POOL_WIDTH), F32)],
        compiler_params=pltpu.CompilerParams(
            dimension_semantics=("arbitrary", "arbitrary"),
            vmem_limit_bytes=_vmem_limit(blocks + (ts + POOL_HALO) * POOL_WIDTH * 4)),
        name="pool",
    )(proj3, pool_w_bf, pool_b, pool_scale)


def _hgrn_kernel(q_ref, z_ref, v_ref, gate_ref, lbl_ref, gn_ref, o_ref, st_ref, *, ts):
    nb = HGRN_BLOCK
    half = V7X_SUBLANES
    dk = HGRN_DK
    nh = HGRN_HEADS
    width = HGRN_WIDTH

    @pl.when(pl.program_id(0) == 0)
    def _():
        st_ref[...] = jnp.zeros(st_ref.shape, F32)

    lg = lbl_ref[...]
    mx = jnp.max(lg, axis=0, keepdims=True)
    ex = jnp.exp(lg - mx)
    lb = ex[0:1, :] / jnp.sum(ex, axis=0, keepdims=True)
    gn = gn_ref[...]

    rowi = lax.broadcasted_iota(jnp.int32, (half, width), 0)
    sel = (lax.broadcasted_iota(jnp.int32, (nb, nb * dk), 0)
           == lax.broadcasted_iota(jnp.int32, (nb, nb * dk), 1) // dk).astype(BF16)
    lane_head = lax.broadcasted_iota(jnp.int32, (nb, nh * nb), 1) // nb

    def block_cumsum(x):
        lo, hi = x[0:half, :], x[half:nb, :]
        sh = 1
        while sh < half:
            r_lo = pltpu.roll(lo, sh, axis=0)
            r_hi = pltpu.roll(hi, sh, axis=0)
            lo, hi = (lo + jnp.where(rowi >= sh, r_lo, 0.0),
                      hi + jnp.where(rowi >= sh, r_hi, r_lo))
            sh *= 2
        return jnp.concatenate([lo, hi + lo], axis=0)

    def front(n, bi):
        r0 = pl.multiple_of(n * nb, nb)
        q = q_ref[bi, pl.ds(r0, nb), :]
        z = z_ref[bi, pl.ds(r0, nb), :]

        f = lb + (1.0 - lb) * _sigmoid(z)
        kk = 1.0 - f
        k_lo, k_hi = kk[0:half, :], kk[half:nb, :]
        b = block_cumsum(jnp.log(f))
        a = (q * jnp.exp(b)).astype(BF16)
        dec = jnp.exp(b[nb - 1:nb, :])

        u_lo = None
        u_hi = None
        cols = []
        for t in range(nb):
            ft = jnp.broadcast_to(f[t:t + 1, :], (half, width))
            qt = jnp.broadcast_to(q[t:t + 1, :], (half, width))
            if t == 0:
                u_lo = jnp.where(rowi == 0, 1.0, 0.0)
            elif t < half:
                u_lo = jnp.where(rowi == t, 1.0, u_lo * ft)
            else:
                u_lo = u_lo * ft
                u_hi = jnp.where(rowi == t - half, 1.0, u_hi * ft if t > half else 0.0)
            w_lo = u_lo * k_lo
            w_hi = u_hi * k_hi if t >= half else jnp.zeros((half, width), F32)
            cols.append(jnp.concatenate([w_lo * qt, w_hi * qt], axis=0).astype(BF16))
        kx = jnp.concatenate([w_lo, w_hi], axis=0).astype(BF16)

        pcat = jnp.concatenate(
            [jnp.concatenate([c[:, h * dk:(h + 1) * dk] for c in cols], axis=1)
             for h in range(nh)], axis=0)
        sc = lax.dot_general(sel, pcat, (((1,), (1,)), ((), ())),
                             preferred_element_type=F32)
        return sc, a, kx, dec

    def back(n, bi, sc, a, kx, dec):
        r0 = pl.multiple_of(n * nb, nb)
        v = v_ref[bi, pl.ds(r0, nb), :]
        gate = gate_ref[bi, pl.ds(r0, nb), :]
        sc_heads = jnp.concatenate(
            [jnp.where(lane_head == h, sc, 0.0) for h in range(nh)], axis=0).astype(BF16)
        v_rows = jnp.concatenate([v[:, h * dk:(h + 1) * dk] for h in range(nh)],
                                 axis=0).astype(BF16)
        o_intra = _dot(sc_heads, v_rows)

        outs = []
        for h in range(nh):
            hs = slice(h * dk, (h + 1) * dk)
            st = st_ref[bi, h]
            o_h = lax.dot_general(a[:, hs], st.astype(BF16), (((1,), (1,)), ((), ())),
                                  preferred_element_type=F32)
            o_h = o_h + o_intra[h * nb:(h + 1) * nb, :]
            ms = jnp.mean(o_h * o_h, axis=-1, keepdims=True)
            outs.append(o_h * lax.rsqrt(ms + RMS_EPS))
            upd = _dot(v[:, hs].T.astype(BF16), kx[:, hs])
            st_ref[bi, h] = st * dec[:, hs] + upd

        o = jnp.concatenate(outs, axis=1) * gn
        o = o * (gate * _sigmoid(gate))
        o_ref[bi, pl.ds(r0, nb), :] = o.astype(BF16)

    seqs = range(q_ref.shape[0])

    def step(n, carry):
        nxt = tuple(front(n + 1, bi) for bi in seqs)
        for bi in seqs:
            back(n, bi, *carry[bi])
        return nxt

    last = lax.fori_loop(0, ts // nb - 1, step, tuple(front(0, bi) for bi in seqs))
    for bi in seqs:
        back(ts // nb - 1, bi, *last[bi])


def _hgrn(proj3, lb_logits, g_norm, ts=512):
    b, s, _ = proj3.shape
    nt = s // ts
    w = HGRN_WIDTH

    def col(c):
        return pl.BlockSpec((b, ts, w), lambda i: (0, i, c))

    blocks = b * (4 * ts * w * 4 + ts * w * 2)
    out = pl.pallas_call(
        functools.partial(_hgrn_kernel, ts=ts),
        grid=(nt,),
        in_specs=[col(1), col(2), col(3), col(4),
                  pl.BlockSpec((2, w), lambda i: (0, 0)),
                  pl.BlockSpec((1, w), lambda i: (0, 0))],
        out_specs=pl.BlockSpec((b, ts, w), lambda i: (0, i, 0)),
        out_shape=jax.ShapeDtypeStruct((b, s, w), BF16),
        scratch_shapes=[pltpu.VMEM((b, HGRN_HEADS, HGRN_DK, HGRN_DK), F32)],
        compiler_params=pltpu.CompilerParams(
            dimension_semantics=("arbitrary",),
            vmem_limit_bytes=_vmem_limit(blocks)),
        name="hgrn",
    )(proj3, proj3, proj3, proj3, lb_logits, g_norm)
    return out.reshape(b * s, w)


def _mix_kernel(a_ref, b_ref, x_ref, wt_ref, wb_ref, g_ref, bt_ref, o_ref, ob_ref):
    mix = _dot(a_ref[...], wt_ref[...]) + _dot(b_ref[...], wb_ref[...])
    y = ALPHA * x_ref[...] + mix
    x1 = _layer_norm(y, g_ref[...], bt_ref[...])
    o_ref[...] = x1
    ob_ref[...] = x1.astype(BF16)


def _mix(a_out, b_out, x2d, w_out_bf, ln_g, ln_b, tm=512):
    t, d = x2d.shape
    hw = a_out.shape[1]
    blocks = 2 * tm * hw * 2 + 2 * tm * d * 4 + tm * d * 2 + d * d * 2
    return pl.pallas_call(
        _mix_kernel,
        grid=(t // tm,),
        in_specs=[pl.BlockSpec((tm, hw), lambda i: (i, 0)),
                  pl.BlockSpec((tm, hw), lambda i: (i, 0)),
                  pl.BlockSpec((tm, d), lambda i: (i, 0)),
                  pl.BlockSpec((hw, d), lambda i: (0, 0)),
                  pl.BlockSpec((hw, d), lambda i: (1, 0)),
                  pl.BlockSpec((1, d), lambda i: (0, 0)),
                  pl.BlockSpec((1, d), lambda i: (0, 0))],
        out_specs=[pl.BlockSpec((tm, d), lambda i: (i, 0)),
                   pl.BlockSpec((tm, d), lambda i: (i, 0))],
        out_shape=[jax.ShapeDtypeStruct((t, d), F32), jax.ShapeDtypeStruct((t, d), BF16)],
        compiler_params=pltpu.CompilerParams(
            dimension_semantics=("arbitrary",),
            vmem_limit_bytes=_vmem_limit(blocks)),
        name="mix",
    )(a_out, b_out, x2d, w_out_bf, w_out_bf, ln_g, ln_b)


def _ffn_kernel(xb_ref, xres_ref, wg_ref, wv_ref, cwg_ref, cwv_ref, cbg_ref, cbv_ref, wd_ref,
                g_ref, bt_ref, o_ref, ug_ref, uv_ref, tail_ref, h_ref,
                *, tm, tiles_per_seq, nf, ns):
    s = pl.program_id(0)
    su = jnp.minimum(s, ns - 1)
    iu = su // nf
    ju = su % nf
    jd = jnp.maximum(s - 1, 0) % nf
    halo = CONV_HALO

    @pl.when(s == 0)
    def _():
        tail_ref[...] = jnp.zeros(tail_ref.shape, F32)
        h_ref[...] = jnp.zeros(h_ref.shape, BF16)

    @pl.when(jd == 0)
    def _():
        o_ref[...] = jnp.zeros(o_ref.shape, F32)

    seq_start = (iu % tiles_per_seq) == 0

    def conv(w_ref, cw_ref, cb_ref, ubuf_ref, slot):
        u = _dot(xb_ref[...], w_ref[...])
        ubuf_ref[0:halo, :] = jnp.where(seq_start, 0.0, tail_ref[ju, slot])
        ubuf_ref[halo:halo + tm, :] = u
        u1 = ubuf_ref[halo - 1:halo - 1 + tm, :]
        u2 = ubuf_ref[halo - 2:halo - 2 + tm, :]
        tail_ref[ju, slot] = ubuf_ref[tm:tm + halo, :]
        cw = cw_ref[...]
        return cb_ref[...] + cw[0:1, :] * u2 + cw[1:2, :] * u1 + cw[2:3, :] * u

    def body(h_prev_ref, h_next_ref):
        gc = conv(wg_ref, cwg_ref, cbg_ref, ug_ref, 0)
        vc = conv(wv_ref, cwv_ref, cbv_ref, uv_ref, 1)
        o_ref[...] += _dot(h_prev_ref[...], wd_ref[...])
        h_next_ref[...] = ((gc * _sigmoid(gc)) * vc).astype(BF16)

    @pl.when(s % 2 == 0)
    def _():
        body(h_ref.at[1], h_ref.at[0])

    @pl.when(s % 2 == 1)
    def _():
        body(h_ref.at[0], h_ref.at[1])

    for c in range(o_ref.shape[1] // FFN_RES_COLS):
        @pl.when(jd == c)
        def _():
            cols = slice(c * FFN_RES_COLS, (c + 1) * FFN_RES_COLS)
            o_ref[:, cols] += ALPHA * xres_ref[...]

    @pl.when(jnp.logical_and(jd == nf - 1, s > 0))
    def _():
        o_ref[...] = _layer_norm(o_ref[...], g_ref[...], bt_ref[...])


def _ffn(x1, x1_bf, w_up_bf, conv_w, conv_b, w_down_bf, ln_g, ln_b, seq, tm=1024, fc=512):
    t, d = x1.shape
    nf = D_FF // fc
    ns = (t // tm) * nf
    n_res = d // FFN_RES_COLS
    assert n_res <= nf

    def up(s):
        return jnp.minimum(s, ns - 1)

    def down(s):
        return jnp.maximum(s - 1, 0)

    blocks = (tm * d * 2 + tm * FFN_RES_COLS * 4 + tm * d * 4 + 2 * d * fc * 2 + fc * d * 2
              + 8 * fc * 4)
    scratch = (2 * (tm + CONV_HALO) * fc * 4 + nf * 2 * CONV_HALO * fc * 4 + 2 * tm * fc * 2)
    return pl.pallas_call(
        functools.partial(_ffn_kernel, tm=tm, tiles_per_seq=seq // tm, nf=nf, ns=ns),
        grid=(ns + 1,),
        in_specs=[pl.BlockSpec((tm, d), lambda s: (up(s) // nf, 0)),
                  pl.BlockSpec((tm, FFN_RES_COLS),
                               lambda s: (down(s) // nf, jnp.minimum(down(s) % nf, n_res - 1))),
                  pl.BlockSpec((d, fc), lambda s: (0, up(s) % nf)),
                  pl.BlockSpec((d, fc), lambda s: (0, up(s) % nf + nf)),
                  pl.BlockSpec((CONV_WIDTH, fc), lambda s: (0, up(s) % nf)),
                  pl.BlockSpec((CONV_WIDTH, fc), lambda s: (0, up(s) % nf + nf)),
                  pl.BlockSpec((1, fc), lambda s: (0, up(s) % nf)),
                  pl.BlockSpec((1, fc), lambda s: (0, up(s) % nf + nf)),
                  pl.BlockSpec((fc, d), lambda s: (down(s) % nf, 0)),
                  pl.BlockSpec((1, d), lambda s: (0, 0)),
                  pl.BlockSpec((1, d), lambda s: (0, 0))],
        out_specs=pl.BlockSpec((tm, d), lambda s: (down(s) // nf, 0)),
        out_shape=jax.ShapeDtypeStruct((t, d), F32),
        scratch_shapes=[pltpu.VMEM((tm + CONV_HALO, fc), F32),
                        pltpu.VMEM((tm + CONV_HALO, fc), F32),
                        pltpu.VMEM((nf, 2, CONV_HALO, fc), F32),
                        pltpu.VMEM((2, tm, fc), BF16)],
        compiler_params=pltpu.CompilerParams(
            dimension_semantics=("arbitrary",),
            vmem_limit_bytes=_vmem_limit(blocks + scratch)),
        name="ffn",
    )(x1_bf, x1, w_up_bf, w_up_bf, conv_w, conv_w, conv_b, conv_b, w_down_bf, ln_g, ln_b)


def kernel(x, w_in, pool_w, pool_b, pool_scale, hgrn_lb_logits, hgrn_g_norm, w_out,
           ln1_g, ln1_b, w_up, conv_w, conv_b, w_down, ln2_g, ln2_b):
    bsz, seq, d = x.shape
    assert (d, w_in.shape[0], hgrn_lb_logits.shape[0]) == (D_MODEL, DEPTH, DEPTH + 1)
    t = bsz * seq
    x2d = x.reshape(t, d)

    proj = _proj(x2d, w_in[0].astype(BF16))
    proj3 = proj.reshape(bsz, seq, IN_COLS)
    a_out = _pool(proj3, pool_w[0].astype(BF16), pool_b[0].reshape(1, POOL_WIDTH),
                  pool_scale[0].reshape(1, POOL_WIDTH))
    b_out = _hgrn(proj3, hgrn_lb_logits, hgrn_g_norm[0].reshape(1, HGRN_WIDTH))
    x1, x1_bf = _mix(a_out, b_out, x2d, w_out[0].astype(BF16),
                     ln1_g[0].reshape(1, d), ln1_b[0].reshape(1, d))
    out = _ffn(x1, x1_bf, w_up[0].astype(BF16), conv_w[0], conv_b[0].reshape(1, 2 * D_FF),
               w_down[0].astype(BF16), ln2_g[0].reshape(1, d), ln2_b[0].reshape(1, d), seq)
    return out.reshape(bsz, seq, d)
```

```python
import functools

import jax
import jax.numpy as jnp
from jax import lax
from jax.experimental import pallas as pl
from jax.experimental.pallas import tpu as pltpu

D_MODEL = 2048
POOL_WIDTH = 1024
POOL_WINDOWS = (2, 4, 8, 16)
POOL_GROUP_DIM = 256
HGRN_WIDTH = 1024
HGRN_HEADS = 8
HGRN_DK = 128
IN_COLS = 5120
D_FF = 5632
CONV_WIDTH = 3
DEPTH = 1
ALPHA = (2.0 * DEPTH) ** 0.25
LN_EPS = 1e-5
RMS_EPS = 1e-6

V7X_LANES = 128
V7X_SUBLANES = 8
V7X_VMEM_BYTES = 64 * 1024 * 1024

HGRN_BLOCK = 16
POOL_HALO = 16
CONV_HALO = V7X_SUBLANES
FFN_RES_COLS = 256

BF16 = jnp.bfloat16
F32 = jnp.float32


def _vmem_limit(block_bytes):
    return int(min(V7X_VMEM_BYTES * 7 // 8, 2 * block_bytes + 16 * 1024 * 1024))


def _dot(a, b):
    return jnp.dot(a, b, preferred_element_type=F32)


def _sigmoid(x):
    return 1.0 / (1.0 + jnp.exp(-x))


def _layer_norm(y, g, b):
    mu = jnp.mean(y, axis=-1, keepdims=True)
    yc = y - mu
    var = jnp.mean(yc * yc, axis=-1, keepdims=True)
    return yc * lax.rsqrt(var + LN_EPS) * g + b


def _proj_kernel(x_ref, w_ref, o_ref, wb_ref):
    @pl.when(pl.program_id(1) == 0)
    def _():
        wb_ref[...] = w_ref[...].astype(BF16)

    o_ref[...] = _dot(x_ref[...].astype(BF16), wb_ref[...])


def _proj(x2d, w, tm=1024, tn=1024):
    t, d = x2d.shape
    n = w.shape[1]
    blocks = tm * d * 4 + d * tn * 4 + tm * tn * 4
    return pl.pallas_call(
        _proj_kernel,
        grid=(n // tn, t // tm),
        in_specs=[pl.BlockSpec((tm, d), lambda j, i: (i, 0)),
                  pl.BlockSpec((d, tn), lambda j, i: (0, j))],
        out_specs=pl.BlockSpec((tm, tn), lambda j, i: (i, j)),
        out_shape=jax.ShapeDtypeStruct((t, n), F32),
        scratch_shapes=[pltpu.VMEM((d, tn), BF16)],
        compiler_params=pltpu.CompilerParams(
            dimension_semantics=("arbitrary", "arbitrary"),
            vmem_limit_bytes=_vmem_limit(blocks + d * tn * 2)),
        name="proj",
    )(x2d, w)


def _pool_kernel(u_ref, w_ref, pb_ref, ps_ref, o_ref, buf_ref, *, ts):
    i = pl.program_id(1)

    @pl.when(i == 0)
    def _():
        buf_ref[0:POOL_HALO, :] = jnp.zeros((POOL_HALO, POOL_WIDTH), F32)

    @pl.when(i > 0)
    def _():
        buf_ref[0:POOL_HALO, :] = buf_ref[ts:ts + POOL_HALO, :]

    buf_ref[POOL_HALO:POOL_HALO + ts, :] = u_ref[0]

    pos = i * ts + lax.broadcasted_iota(jnp.int32, (ts, POOL_GROUP_DIM), 0)
    for gi, w in enumerate(POOL_WINDOWS):
        cols = slice(gi * POOL_GROUP_DIM, (gi + 1) * POOL_GROUP_DIM)
        cur = buf_ref[POOL_HALO:POOL_HALO + ts, cols]
        win = cur
        for j in range(1, w):
            win = win + buf_ref[POOL_HALO - j:POOL_HALO - j + ts, cols]
        cnt = jnp.minimum(pos + 1, w).astype(F32)
        pooled = win / cnt - cur
        y = _dot(pooled.astype(BF16), w_ref[gi]) + pb_ref[:, cols]
        o_ref[:, cols] = (y * ps_ref[:, cols]).astype(BF16)


def _pool(proj3, pool_w_bf, pool_b, pool_scale, ts=512):
    b, s, _ = proj3.shape
    nt = s // ts
    blocks = ts * POOL_WIDTH * 4 + 4 * 256 * 256 * 2 + ts * POOL_WIDTH * 2
    return pl.pallas_call(
        functools.partial(_pool_kernel, ts=ts),
        grid=(b, nt),
        in_specs=[pl.BlockSpec((1, ts, POOL_WIDTH), lambda bi, i: (bi, i, 0)),
                  pl.BlockSpec((4, 256, 256), lambda bi, i: (0, 0, 0)),
                  pl.BlockSpec((1, POOL_WIDTH), lambda bi, i: (0, 0)),
                  pl.BlockSpec((1, POOL_WIDTH), lambda bi, i: (0, 0))],
        out_specs=pl.BlockSpec((ts, POOL_WIDTH), lambda bi, i: (bi * nt + i, 0)),
        out_shape=jax.ShapeDtypeStruct((b * s, POOL_WIDTH), BF16),
        scratch_shapes=[pltpu.VMEM((ts + POOL_HALO, POOL_WIDTH), F32)],
        compiler_params=pltpu.CompilerParams(
            dimension_semantics=("arbitrary", "arbitrary"),
            vmem_limit_bytes=_vmem_limit(blocks + (ts + POOL_HALO) * POOL_WIDTH * 4)),
        name="pool",
    )(proj3, pool_w_bf, pool_b, pool_scale)


def _hgrn_kernel(q_ref, z_ref, v_ref, gate_ref, lbl_ref, gn_ref, o_ref, st_ref, *, ts):
    nb = HGRN_BLOCK
    half = V7X_SUBLANES
    dk = HGRN_DK
    nh = HGRN_HEADS
    width = HGRN_WIDTH

    @pl.when(pl.program_id(0) == 0)
    def _():
        st_ref[...] = jnp.zeros(st_ref.shape, F32)

    lg = lbl_ref[...]
    mx = jnp.max(lg, axis=0, keepdims=True)
    ex = jnp.exp(lg - mx)
    lb = ex[0:1, :] / jnp.sum(ex, axis=0, keepdims=True)
    gn = gn_ref[...]

    rowi = lax.broadcasted_iota(jnp.int32, (half, width), 0)
    sel = (lax.broadcasted_iota(jnp.int32, (nb, nb * dk), 0)
           == lax.broadcasted_iota(jnp.int32, (nb, nb * dk), 1) // dk).astype(BF16)
    lane_head = lax.broadcasted_iota(jnp.int32, (nb, nh * nb), 1) // nb

    def block_cumsum(x):
        lo, hi = x[0:half, :], x[half:nb, :]
        sh = 1
        while sh < half:
            r_lo = pltpu.roll(lo, sh, axis=0)
            r_hi = pltpu.roll(hi, sh, axis=0)
            lo, hi = (lo + jnp.where(rowi >= sh, r_lo, 0.0),
                      hi + jnp.where(rowi >= sh, r_hi, r_lo))
            sh *= 2
        return jnp.concatenate([lo, hi + lo], axis=0)

    def front(n, bi):
        r0 = pl.multiple_of(n * nb, nb)
        q = q_ref[bi, pl.ds(r0, nb), :]
        z = z_ref[bi, pl.ds(r0, nb), :]

        f = lb + (1.0 - lb) * _sigmoid(z)
        kk = 1.0 - f
        k_lo, k_hi = kk[0:half, :], kk[half:nb, :]
        b = block_cumsum(jnp.log(f))
        a = (q * jnp.exp(b)).astype(BF16)
        dec = jnp.exp(b[nb - 1:nb, :])

        u_lo = None
        u_hi = None
        cols = []
        for t in range(nb):
            ft = jnp.broadcast_to(f[t:t + 1, :], (half, width))
            qt = jnp.broadcast_to(q[t:t + 1, :], (half, width))
            if t == 0:
                u_lo = jnp.where(rowi == 0, 1.0, 0.0)
            elif t < half:
                u_lo = jnp.where(rowi == t, 1.0, u_lo * ft)
            else:
                u_lo = u_lo * ft
                u_hi = jnp.where(rowi == t - half, 1.0, u_hi * ft if t > half else 0.0)
            w_lo = u_lo * k_lo
            w_hi = u_hi * k_hi if t >= half else jnp.zeros((half, width), F32)
            cols.append(jnp.concatenate([w_lo * qt, w_hi * qt], axis=0).astype(BF16))
        kx = jnp.concatenate([w_lo, w_hi], axis=0).astype(BF16)

        pcat = jnp.concatenate(
            [jnp.concatenate([c[:, h * dk:(h + 1) * dk] for c in cols], axis=1)
             for h in range(nh)], axis=0)
        sc = lax.dot_general(sel, pcat, (((1,), (1,)), ((), ())),
                             preferred_element_type=F32)
        return sc, a, kx, dec

    def back(n, bi, sc, a, kx, dec):
        r0 = pl.multiple_of(n * nb, nb)
        v = v_ref[bi, pl.ds(r0, nb), :]
        gate = gate_ref[bi, pl.ds(r0, nb), :]
        sc_heads = jnp.concatenate(
            [jnp.where(lane_head == h, sc, 0.0) for h in range(nh)], axis=0).astype(BF16)
        v_rows = jnp.concatenate([v[:, h * dk:(h + 1) * dk] for h in range(nh)],
                                 axis=0).astype(BF16)
        o_intra = _dot(sc_heads, v_rows)

        outs = []
        for h in range(nh):
            hs = slice(h * dk, (h + 1) * dk)
            st = st_ref[bi, h]
            o_h = lax.dot_general(a[:, hs], st.astype(BF16), (((1,), (1,)), ((), ())),
                                  preferred_element_type=F32)
            o_h = o_h + o_intra[h * nb:(h + 1) * nb, :]
            ms = jnp.mean(o_h * o_h, axis=-1, keepdims=True)
            outs.append(o_h * lax.rsqrt(ms + RMS_EPS))
            upd = _dot(v[:, hs].T.astype(BF16), kx[:, hs])
            st_ref[bi, h] = st * dec[:, hs] + upd

        o = jnp.concatenate(outs, axis=1) * gn
        o = o * (gate * _sigmoid(gate))
        o_ref[bi, pl.ds(r0, nb), :] = o.astype(BF16)

    seqs = range(q_ref.shape[0])

    def step(n, carry):
        nxt = tuple(front(n + 1, bi) for bi in seqs)
        for bi in seqs:
            back(n, bi, *carry[bi])
        return nxt

    last = lax.fori_loop(0, ts // nb - 1, step, tuple(front(0, bi) for bi in seqs))
    for bi in seqs:
        back(ts // nb - 1, bi, *last[bi])


def _hgrn(proj3, lb_logits, g_norm, ts=512):
    b, s, _ = proj3.shape
    nt = s // ts
    w = HGRN_WIDTH

    def col(c):
        return pl.BlockSpec((b, ts, w), lambda i: (0, i, c))

    blocks = b * (4 * ts * w * 4 + ts * w * 2)
    out = pl.pallas_call(
        functools.partial(_hgrn_kernel, ts=ts),
        grid=(nt,),
        in_specs=[col(1), col(2), col(3), col(4),
                  pl.BlockSpec((2, w), lambda i: (0, 0)),
                  pl.BlockSpec((1, w), lambda i: (0, 0))],
        out_specs=pl.BlockSpec((b, ts, w), lambda i: (0, i, 0)),
        out_shape=jax.ShapeDtypeStruct((b, s, w), BF16),
        scratch_shapes=[pltpu.VMEM((b, HGRN_HEADS, HGRN_DK, HGRN_DK), F32)],
        compiler_params=pltpu.CompilerParams(
            dimension_semantics=("arbitrary",),
            vmem_limit_bytes=_vmem_limit(blocks)),
        name="hgrn",
    )(proj3, proj3, proj3, proj3, lb_logits, g_norm)
    return out.reshape(b * s, w)


def _mix_kernel(a_ref, b_ref, x_ref, wt_ref, wb_ref, g_ref, bt_ref, o_ref, ob_ref):
    mix = _dot(a_ref[...], wt_ref[...]) + _dot(b_ref[...], wb_ref[...])
    y = ALPHA * x_ref[...] + mix
    x1 = _layer_norm(y, g_ref[...], bt_ref[...])
    o_ref[...] = x1
    ob_ref[...] = x1.astype(BF16)


def _mix(a_out, b_out, x2d, w_out_bf, ln_g, ln_b, tm=512):
    t, d = x2d.shape
    hw = a_out.shape[1]
    blocks = 2 * tm * hw * 2 + 2 * tm * d * 4 + tm * d * 2 + d * d * 2
    return pl.pallas_call(
        _mix_kernel,
        grid=(t // tm,),
        in_specs=[pl.BlockSpec((tm, hw), lambda i: (i, 0)),
                  pl.BlockSpec((tm, hw), lambda i: (i, 0)),
                  pl.BlockSpec((tm, d), lambda i: (i, 0)),
                  pl.BlockSpec((hw, d), lambda i: (0, 0)),
                  pl.BlockSpec((hw, d), lambda i: (1, 0)),
                  pl.BlockSpec((1, d), lambda i: (0, 0)),
                  pl.BlockSpec((1, d), lambda i: (0, 0))],
        out_specs=[pl.BlockSpec((tm, d), lambda i: (i, 0)),
                   pl.BlockSpec((tm, d), lambda i: (i, 0))],
        out_shape=[jax.ShapeDtypeStruct((t, d), F32), jax.ShapeDtypeStruct((t, d), BF16)],
        compiler_params=pltpu.CompilerParams(
            dimension_semantics=("arbitrary",),
            vmem_limit_bytes=_vmem_limit(blocks)),
        name="mix",
    )(a_out, b_out, x2d, w_out_bf, w_out_bf, ln_g, ln_b)


def _ffn_kernel(xb_ref, xres_ref, wg_ref, wv_ref, cwg_ref, cwv_ref, cbg_ref, cbv_ref, wd_ref,
                g_ref, bt_ref, o_ref, ug_ref, uv_ref, tail_ref, h_ref,
                *, tm, tiles_per_seq, nf, ns):
    s = pl.program_id(0)
    su = jnp.minimum(s, ns - 1)
    iu = su // nf
    ju = su % nf
    jd = jnp.maximum(s - 1, 0) % nf
    halo = CONV_HALO

    @pl.when(s == 0)
    def _():
        tail_ref[...] = jnp.zeros(tail_ref.shape, F32)
        h_ref[...] = jnp.zeros(h_ref.shape, BF16)

    @pl.when(jd == 0)
    def _():
        o_ref[...] = jnp.zeros(o_ref.shape, F32)

    seq_start = (iu % tiles_per_seq) == 0

    def conv(w_ref, cw_ref, cb_ref, ubuf_ref, slot):
        u = _dot(xb_ref[...], w_ref[...])
        ubuf_ref[0:halo, :] = jnp.where(seq_start, 0.0, tail_ref[ju, slot])
        ubuf_ref[halo:halo + tm, :] = u
        u1 = ubuf_ref[halo - 1:halo - 1 + tm, :]
        u2 = ubuf_ref[halo - 2:halo - 2 + tm, :]
        tail_ref[ju, slot] = ubuf_ref[tm:tm + halo, :]
        cw = cw_ref[...]
        return cb_ref[...] + cw[0:1, :] * u2 + cw[1:2, :] * u1 + cw[2:3, :] * u

    def body(h_prev_ref, h_next_ref):
        gc = conv(wg_ref, cwg_ref, cbg_ref, ug_ref, 0)
        vc = conv(wv_ref, cwv_ref, cbv_ref, uv_ref, 1)
        o_ref[...] += _dot(h_prev_ref[...], wd_ref[...])
        h_next_ref[...] = ((gc * _sigmoid(gc)) * vc).astype(BF16)

    @pl.when(s % 2 == 0)
    def _():
        body(h_ref.at[1], h_ref.at[0])

    @pl.when(s % 2 == 1)
    def _():
        body(h_ref.at[0], h_ref.at[1])

    for c in range(o_ref.shape[1] // FFN_RES_COLS):
        @pl.when(jd == c)
        def _():
            cols = slice(c * FFN_RES_COLS, (c + 1) * FFN_RES_COLS)
            o_ref[:, cols] += ALPHA * xres_ref[...]

    @pl.when(jnp.logical_and(jd == nf - 1, s > 0))
    def _():
        o_ref[...] = _layer_norm(o_ref[...], g_ref[...], bt_ref[...])


def _ffn(x1, x1_bf, w_up_bf, conv_w, conv_b, w_down_bf, ln_g, ln_b, seq, tm=1024, fc=512):
    t, d = x1.shape
    nf = D_FF // fc
    ns = (t // tm) * nf
    n_res = d // FFN_RES_COLS
    assert n_res <= nf

    def up(s):
        return jnp.minimum(s, ns - 1)

    def down(s):
        return jnp.maximum(s - 1, 0)

    blocks = (tm * d * 2 + tm * FFN_RES_COLS * 4 + tm * d * 4 + 2 * d * fc * 2 + fc * d * 2
              + 8 * fc * 4)
    scratch = (2 * (tm + CONV_HALO) * fc * 4 + nf * 2 * CONV_HALO * fc * 4 + 2 * tm * fc * 2)
    return pl.pallas_call(
        functools.partial(_ffn_kernel, tm=tm, tiles_per_seq=seq // tm, nf=nf, ns=ns),
        grid=(ns + 1,),
        in_specs=[pl.BlockSpec((tm, d), lambda s: (up(s) // nf, 0)),
                  pl.BlockSpec((tm, FFN_RES_COLS),
                               lambda s: (down(s) // nf, jnp.minimum(down(s) % nf, n_res - 1))),
                  pl.BlockSpec((d, fc), lambda s: (0, up(s) % nf)),
                  pl.BlockSpec((d, fc), lambda s: (0, up(s) % nf + nf)),
                  pl.BlockSpec((CONV_WIDTH, fc), lambda s: (0, up(s) % nf)),
                  pl.BlockSpec((CONV_WIDTH, fc), lambda s: (0, up(s) % nf + nf)),
                  pl.BlockSpec((1, fc), lambda s: (0, up(s) % nf)),
                  pl.BlockSpec((1, fc), lambda s: (0, up(s) % nf + nf)),
                  pl.BlockSpec((fc, d), lambda s: (down(s) % nf, 0)),
                  pl.BlockSpec((1, d), lambda s: (0, 0)),
                  pl.BlockSpec((1, d), lambda s: (0, 0))],
        out_specs=pl.BlockSpec((tm, d), lambda s: (down(s) // nf, 0)),
        out_shape=jax.ShapeDtypeStruct((t, d), F32),
        scratch_shapes=[pltpu.VMEM((tm + CONV_HALO, fc), F32),
                        pltpu.VMEM((tm + CONV_HALO, fc), F32),
                        pltpu.VMEM((nf, 2, CONV_HALO, fc), F32),
                        pltpu.VMEM((2, tm, fc), BF16)],
        compiler_params=pltpu.CompilerParams(
            dimension_semantics=("arbitrary",),
            vmem_limit_bytes=_vmem_limit(blocks + scratch)),
        name="ffn",
    )(x1_bf, x1, w_up_bf, w_up_bf, conv_w, conv_w, conv_b, conv_b, w_down_bf, ln_g, ln_b)


def kernel(x, w_in, pool_w, pool_b, pool_scale, hgrn_lb_logits, hgrn_g_norm, w_out,
           ln1_g, ln1_b, w_up, conv_w, conv_b, w_down, ln2_g, ln2_b):
    bsz, seq, d = x.shape
    assert (d, w_in.shape[0], hgrn_lb_logits.shape[0]) == (D_MODEL, DEPTH, DEPTH + 1)
    t = bsz * seq
    x2d = x.reshape(t, d)

    proj = _proj(x2d, w_in[0])
    proj3 = proj.reshape(bsz, seq, IN_COLS)
    a_out = _pool(proj3, pool_w[0].astype(BF16), pool_b[0].reshape(1, POOL_WIDTH),
                  pool_scale[0].reshape(1, POOL_WIDTH))
    b_out = _hgrn(proj3, hgrn_lb_logits, hgrn_g_norm[0].reshape(1, HGRN_WIDTH))
    x1, x1_bf = _mix(a_out, b_out, x2d, w_out[0].astype(BF16),
                     ln1_g[0].reshape(1, d), ln1_b[0].reshape(1, d))
    out = _ffn(x1, x1_bf, w_up[0].astype(BF16), conv_w[0], conv_b[0].reshape(1, 2 * D_FF),
               w_down[0].astype(BF16), ln2_g[0].reshape(1, d), ln2_b[0].reshape(1, d), seq)
    return out.reshape(bsz, seq, d)
```

```python
import functools

import jax
import jax.numpy as jnp
from jax import lax
from jax.experimental import pallas as pl
from jax.experimental.pallas import tpu as pltpu

D_MODEL = 2048
POOL_WIDTH = 1024
POOL_WINDOWS = (2, 4, 8, 16)
POOL_GROUP_DIM = 256
HGRN_WIDTH = 1024
HGRN_HEADS = 8
HGRN_DK = 128
IN_COLS = 5120
D_FF = 5632
CONV_WIDTH = 3
DEPTH = 1
ALPHA = (2.0 * DEPTH) ** 0.25
LN_EPS = 1e-5
RMS_EPS = 1e-6

V7X_LANES = 128
V7X_SUBLANES = 8
V7X_VMEM_BYTES = 64 * 1024 * 1024

HGRN_BLOCK = 16
POOL_HALO = 16
CONV_HALO = V7X_SUBLANES
FFN_RES_COLS = 256

BF16 = jnp.bfloat16
F32 = jnp.float32


def _vmem_limit(block_bytes):
    return int(min(V7X_VMEM_BYTES * 7 // 8, 2 * block_bytes + 16 * 1024 * 1024))


def _dot(a, b):
    return jnp.dot(a, b, preferred_element_type=F32)


def _sigmoid(x):
    return 1.0 / (1.0 + jnp.exp(-x))


def _layer_norm(y, g, b):
    mu = jnp.mean(y, axis=-1, keepdims=True)
    yc = y - mu
    var = jnp.mean(yc * yc, axis=-1, keepdims=True)
    return yc * lax.rsqrt(var + LN_EPS) * g + b


def _proj_kernel(x_ref, w_ref, o_ref, wb_ref):
    @pl.when(pl.program_id(1) == 0)
    def _():
        wb_ref[...] = w_ref[...].astype(BF16)

    o_ref[...] = _dot(x_ref[...].astype(BF16), wb_ref[...])


def _proj(x2d, w, tm=1024, tn=1024):
    t, d = x2d.shape
    n = w.shape[1]
    blocks = tm * d * 4 + d * tn * 4 + tm * tn * 4
    return pl.pallas_call(
        _proj_kernel,
        grid=(n // tn, t // tm),
        in_specs=[pl.BlockSpec((tm, d), lambda j, i: (i, 0)),
                  pl.BlockSpec((d, tn), lambda j, i: (0, j))],
        out_specs=pl.BlockSpec((tm, tn), lambda j, i: (i, j)),
        out_shape=jax.ShapeDtypeStruct((t, n), F32),
        scratch_shapes=[pltpu.VMEM((d, tn), BF16)],
        compiler_params=pltpu.CompilerParams(
            dimension_semantics=("arbitrary", "arbitrary"),
            vmem_limit_bytes=_vmem_limit(blocks + d * tn * 2)),
        name="proj",
    )(x2d, w)


def _pool_kernel(u_ref, w_ref, pb_ref, ps_ref, o_ref, buf_ref, *, ts):
    i = pl.program_id(1)

    @pl.when(i == 0)
    def _():
        buf_ref[0:POOL_HALO, :] = jnp.zeros((POOL_HALO, POOL_WIDTH), F32)

    @pl.when(i > 0)
    def _():
        buf_ref[0:POOL_HALO, :] = buf_ref[ts:ts + POOL_HALO, :]

    buf_ref[POOL_HALO:POOL_HALO + ts, :] = u_ref[0]

    pos = i * ts + lax.broadcasted_iota(jnp.int32, (ts, POOL_GROUP_DIM), 0)
    for gi, w in enumerate(POOL_WINDOWS):
        cols = slice(gi * POOL_GROUP_DIM, (gi + 1) * POOL_GROUP_DIM)
        cur = buf_ref[POOL_HALO:POOL_HALO + ts, cols]
        win = cur
        for j in range(1, w):
            win = win + buf_ref[POOL_HALO - j:POOL_HALO - j + ts, cols]
        cnt = jnp.minimum(pos + 1, w).astype(F32)
        pooled = win / cnt - cur
        y = _dot(pooled.astype(BF16), w_ref[gi]) + pb_ref[:, cols]
        o_ref[:, cols] = (y * ps_ref[:, cols]).astype(BF16)


def _pool(proj3, pool_w_bf, pool_b, pool_scale, ts=512):
    b, s, _ = proj3.shape
    nt = s // ts
    blocks = ts * POOL_WIDTH * 4 + 4 * 256 * 256 * 2 + ts * POOL_WIDTH * 2
    return pl.pallas_call(
        functools.partial(_pool_kernel, ts=ts),
        grid=(b, nt),
        in_specs=[pl.BlockSpec((1, ts, POOL_WIDTH), lambda bi, i: (bi, i, 0)),
                  pl.BlockSpec((4, 256, 256), lambda bi, i: (0, 0, 0)),
                  pl.BlockSpec((1, POOL_WIDTH), lambda bi, i: (0, 0)),
                  pl.BlockSpec((1, POOL_WIDTH), lambda bi, i: (0, 0))],
        out_specs=pl.BlockSpec((ts, POOL_WIDTH), lambda bi, i: (bi * nt + i, 0)),
        out_shape=jax.ShapeDtypeStruct((b * s, POOL_WIDTH), BF16),
        scratch_shapes=[pltpu.VMEM((ts + POOL_HALO, POOL_WIDTH), F32)],
        compiler_params=pltpu.CompilerParams(
            dimension_semantics=("arbitrary", "arbitrary"),
            vmem_limit_bytes=_vmem_limit(blocks + (ts + POOL_HALO) * POOL_WIDTH * 4)),
        name="pool",
    )(proj3, pool_w_bf, pool_b, pool_scale)


def _hgrn_kernel(q_ref, z_ref, v_ref, gate_ref, lbl_ref, gn_ref, wu_ref, wd_ref, wo_ref,
                 o_ref, wub_ref, wdb_ref, wob_ref, st_ref, *, ts):
    nb = HGRN_BLOCK
    half = V7X_SUBLANES
    dk = HGRN_DK
    nh = HGRN_HEADS
    width = HGRN_WIDTH

    @pl.when(pl.program_id(0) == 0)
    def _():
        st_ref[...] = jnp.zeros(st_ref.shape, F32)

    wub_ref[...] = wu_ref[...].astype(BF16)
    wdb_ref[...] = wd_ref[...].astype(BF16)
    wob_ref[...] = wo_ref[...].astype(BF16)

    lg = lbl_ref[...]
    mx = jnp.max(lg, axis=0, keepdims=True)
    ex = jnp.exp(lg - mx)
    lb = ex[0:1, :] / jnp.sum(ex, axis=0, keepdims=True)
    gn = gn_ref[...]

    rowi = lax.broadcasted_iota(jnp.int32, (half, width), 0)
    sel = (lax.broadcasted_iota(jnp.int32, (nb, nb * dk), 0)
           == lax.broadcasted_iota(jnp.int32, (nb, nb * dk), 1) // dk).astype(BF16)
    lane_head = lax.broadcasted_iota(jnp.int32, (nb, nh * nb), 1) // nb

    def block_cumsum(x):
        lo, hi = x[0:half, :], x[half:nb, :]
        sh = 1
        while sh < half:
            r_lo = pltpu.roll(lo, sh, axis=0)
            r_hi = pltpu.roll(hi, sh, axis=0)
            lo, hi = (lo + jnp.where(rowi >= sh, r_lo, 0.0),
                      hi + jnp.where(rowi >= sh, r_hi, r_lo))
            sh *= 2
        return jnp.concatenate([lo, hi + lo], axis=0)

    def front(n, bi):
        r0 = pl.multiple_of(n * nb, nb)
        q = q_ref[bi, pl.ds(r0, nb), :]
        z = z_ref[bi, pl.ds(r0, nb), :]

        f = lb + (1.0 - lb) * _sigmoid(z)
        kk = 1.0 - f
        k_lo, k_hi = kk[0:half, :], kk[half:nb, :]
        b = block_cumsum(jnp.log(f))
        a = (q * jnp.exp(b)).astype(BF16)
        dec = jnp.exp(b[nb - 1:nb, :])

        u_lo = None
        u_hi = None
        cols = []
        for t in range(nb):
            ft = jnp.broadcast_to(f[t:t + 1, :], (half, width))
            qt = jnp.broadcast_to(q[t:t + 1, :], (half, width))
            if t == 0:
                u_lo = jnp.where(rowi == 0, 1.0, 0.0)
            elif t < half:
                u_lo = jnp.where(rowi == t, 1.0, u_lo * ft)
            else:
                u_lo = u_lo * ft
                u_hi = jnp.where(rowi == t - half, 1.0, u_hi * ft if t > half else 0.0)
            w_lo = u_lo * k_lo
            w_hi = u_hi * k_hi if t >= half else jnp.zeros((half, width), F32)
            cols.append(jnp.concatenate([w_lo * qt, w_hi * qt], axis=0).astype(BF16))
        kx = jnp.concatenate([w_lo, w_hi], axis=0).astype(BF16)

        pcat = jnp.concatenate(
            [jnp.concatenate([c[:, h * dk:(h + 1) * dk] for c in cols], axis=1)
             for h in range(nh)], axis=0)
        sc = lax.dot_general(sel, pcat, (((1,), (1,)), ((), ())),
                             preferred_element_type=F32)
        return sc, a, kx, dec

    def back(n, bi, sc, a, kx, dec):
        r0 = pl.multiple_of(n * nb, nb)
        v = v_ref[bi, pl.ds(r0, nb), :]
        gate = gate_ref[bi, pl.ds(r0, nb), :]
        sc_heads = jnp.concatenate(
            [jnp.where(lane_head == h, sc, 0.0) for h in range(nh)], axis=0).astype(BF16)
        v_rows = jnp.concatenate([v[:, h * dk:(h + 1) * dk] for h in range(nh)],
                                 axis=0).astype(BF16)
        o_intra = _dot(sc_heads, v_rows)

        outs = []
        for h in range(nh):
            hs = slice(h * dk, (h + 1) * dk)
            st = st_ref[bi, h]
            o_h = lax.dot_general(a[:, hs], st.astype(BF16), (((1,), (1,)), ((), ())),
                                  preferred_element_type=F32)
            o_h = o_h + o_intra[h * nb:(h + 1) * nb, :]
            ms = jnp.mean(o_h * o_h, axis=-1, keepdims=True)
            outs.append(o_h * lax.rsqrt(ms + RMS_EPS))
            upd = _dot(v[:, hs].T.astype(BF16), kx[:, hs])
            st_ref[bi, h] = st * dec[:, hs] + upd

        o = jnp.concatenate(outs, axis=1) * gn
        o = o * (gate * _sigmoid(gate))
        o_ref[bi, pl.ds(r0, nb), :] = o.astype(BF16)

    seqs = range(q_ref.shape[0])

    def step(n, carry):
        nxt = tuple(front(n + 1, bi) for bi in seqs)
        for bi in seqs:
            back(n, bi, *carry[bi])
        return nxt

    last = lax.fori_loop(0, ts // nb - 1, step, tuple(front(0, bi) for bi in seqs))
    for bi in seqs:
        back(ts // nb - 1, bi, *last[bi])


def _hgrn(proj3, lb_logits, g_norm, side_weights, ts=256):
    b, s, _ = proj3.shape
    nt = s // ts
    w = HGRN_WIDTH

    def col(c):
        return pl.BlockSpec((b, ts, w), lambda i: (0, i, c))

    def slab(wt):
        assert wt.shape[0] % (nt * 2 * V7X_SUBLANES) == 0
        return pl.BlockSpec((wt.shape[0] // nt, wt.shape[1]), lambda i: (i, 0))

    side_elems = sum(wt.size for wt in side_weights) // nt
    blocks = b * (4 * ts * w * 4 + ts * w * 2) + side_elems * (4 + 2)
    out, *side_bf = pl.pallas_call(
        functools.partial(_hgrn_kernel, ts=ts),
        grid=(nt,),
        in_specs=[col(1), col(2), col(3), col(4),
                  pl.BlockSpec((2, w), lambda i: (0, 0)),
                  pl.BlockSpec((1, w), lambda i: (0, 0))] + [slab(wt) for wt in side_weights],
        out_specs=[pl.BlockSpec((b, ts, w), lambda i: (0, i, 0))]
        + [slab(wt) for wt in side_weights],
        out_shape=[jax.ShapeDtypeStruct((b, s, w), BF16)]
        + [jax.ShapeDtypeStruct(wt.shape, BF16) for wt in side_weights],
        scratch_shapes=[pltpu.VMEM((b, HGRN_HEADS, HGRN_DK, HGRN_DK), F32)],
        compiler_params=pltpu.CompilerParams(
            dimension_semantics=("arbitrary",),
            vmem_limit_bytes=_vmem_limit(blocks)),
        name="hgrn",
    )(proj3, proj3, proj3, proj3, lb_logits, g_norm, *side_weights)
    return out.reshape(b * s, w), side_bf


def _mix_kernel(a_ref, b_ref, x_ref, wt_ref, wb_ref, g_ref, bt_ref, o_ref, ob_ref):
    mix = _dot(a_ref[...], wt_ref[...]) + _dot(b_ref[...], wb_ref[...])
    y = ALPHA * x_ref[...] + mix
    x1 = _layer_norm(y, g_ref[...], bt_ref[...])
    o_ref[...] = x1
    ob_ref[...] = x1.astype(BF16)


def _mix(a_out, b_out, x2d, w_out_bf, ln_g, ln_b, tm=512):
    t, d = x2d.shape
    hw = a_out.shape[1]
    blocks = 2 * tm * hw * 2 + 2 * tm * d * 4 + tm * d * 2 + d * d * 2
    return pl.pallas_call(
        _mix_kernel,
        grid=(t // tm,),
        in_specs=[pl.BlockSpec((tm, hw), lambda i: (i, 0)),
                  pl.BlockSpec((tm, hw), lambda i: (i, 0)),
                  pl.BlockSpec((tm, d), lambda i: (i, 0)),
                  pl.BlockSpec((hw, d), lambda i: (0, 0)),
                  pl.BlockSpec((hw, d), lambda i: (1, 0)),
                  pl.BlockSpec((1, d), lambda i: (0, 0)),
                  pl.BlockSpec((1, d), lambda i: (0, 0))],
        out_specs=[pl.BlockSpec((tm, d), lambda i: (i, 0)),
                   pl.BlockSpec((tm, d), lambda i: (i, 0))],
        out_shape=[jax.ShapeDtypeStruct((t, d), F32), jax.ShapeDtypeStruct((t, d), BF16)],
        compiler_params=pltpu.CompilerParams(
            dimension_semantics=("arbitrary",),
            vmem_limit_bytes=_vmem_limit(blocks)),
        name="mix",
    )(a_out, b_out, x2d, w_out_bf, w_out_bf, ln_g, ln_b)


def _ffn_kernel(xb_ref, xres_ref, wg_ref, wv_ref, cwg_ref, cwv_ref, cbg_ref, cbv_ref, wd_ref,
                g_ref, bt_ref, o_ref, ug_ref, uv_ref, tail_ref, h_ref,
                *, tm, tiles_per_seq, nf, ns):
    s = pl.program_id(0)
    su = jnp.minimum(s, ns - 1)
    iu = su // nf
    ju = su % nf
    jd = jnp.maximum(s - 1, 0) % nf
    halo = CONV_HALO

    @pl.when(s == 0)
    def _():
        tail_ref[...] = jnp.zeros(tail_ref.shape, F32)
        h_ref[...] = jnp.zeros(h_ref.shape, BF16)

    @pl.when(jd == 0)
    def _():
        o_ref[...] = jnp.zeros(o_ref.shape, F32)

    seq_start = (iu % tiles_per_seq) == 0

    def conv(w_ref, cw_ref, cb_ref, ubuf_ref, slot):
        u = _dot(xb_ref[...], w_ref[...])
        ubuf_ref[0:halo, :] = jnp.where(seq_start, 0.0, tail_ref[ju, slot])
        ubuf_ref[halo:halo + tm, :] = u
        u1 = ubuf_ref[halo - 1:halo - 1 + tm, :]
        u2 = ubuf_ref[halo - 2:halo - 2 + tm, :]
        tail_ref[ju, slot] = ubuf_ref[tm:tm + halo, :]
        cw = cw_ref[...]
        return cb_ref[...] + cw[0:1, :] * u2 + cw[1:2, :] * u1 + cw[2:3, :] * u

    def body(h_prev_ref, h_next_ref):
        gc = conv(wg_ref, cwg_ref, cbg_ref, ug_ref, 0)
        vc = conv(wv_ref, cwv_ref, cbv_ref, uv_ref, 1)
        o_ref[...] += _dot(h_prev_ref[...], wd_ref[...])
        h_next_ref[...] = ((gc * _sigmoid(gc)) * vc).astype(BF16)

    @pl.when(s % 2 == 0)
    def _():
        body(h_ref.at[1], h_ref.at[0])

    @pl.when(s % 2 == 1)
    def _():
        body(h_ref.at[0], h_ref.at[1])

    for c in range(o_ref.shape[1] // FFN_RES_COLS):
        @pl.when(jd == c)
        def _():
            cols = slice(c * FFN_RES_COLS, (c + 1) * FFN_RES_COLS)
            o_ref[:, cols] += ALPHA * xres_ref[...]

    @pl.when(jnp.logical_and(jd == nf - 1, s > 0))
    def _():
        o_ref[...] = _layer_norm(o_ref[...], g_ref[...], bt_ref[...])


def _ffn(x1, x1_bf, w_up_bf, conv_w, conv_b, w_down_bf, ln_g, ln_b, seq, tm=1024, fc=512):
    t, d = x1.shape
    nf = D_FF // fc
    ns = (t // tm) * nf
    n_res = d // FFN_RES_COLS
    assert n_res <= nf

    def up(s):
        return jnp.minimum(s, ns - 1)

    def down(s):
        return jnp.maximum(s - 1, 0)

    blocks = (tm * d * 2 + tm * FFN_RES_COLS * 4 + tm * d * 4 + 2 * d * fc * 2 + fc * d * 2
              + 8 * fc * 4)
    scratch = (2 * (tm + CONV_HALO) * fc * 4 + nf * 2 * CONV_HALO * fc * 4 + 2 * tm * fc * 2)
    return pl.pallas_call(
        functools.partial(_ffn_kernel, tm=tm, tiles_per_seq=seq // tm, nf=nf, ns=ns),
        grid=(ns + 1,),
        in_specs=[pl.BlockSpec((tm, d), lambda s: (up(s) // nf, 0)),
                  pl.BlockSpec((tm, FFN_RES_COLS),
                               lambda s: (down(s) // nf, jnp.minimum(down(s) % nf, n_res - 1))),
                  pl.BlockSpec((d, fc), lambda s: (0, up(s) % nf)),
                  pl.BlockSpec((d, fc), lambda s: (0, up(s) % nf + nf)),
                  pl.BlockSpec((CONV_WIDTH, fc), lambda s: (0, up(s) % nf)),
                  pl.BlockSpec((CONV_WIDTH, fc), lambda s: (0, up(s) % nf + nf)),
                  pl.BlockSpec((1, fc), lambda s: (0, up(s) % nf)),
                  pl.BlockSpec((1, fc), lambda s: (0, up(s) % nf + nf)),
                  pl.BlockSpec((fc, d), lambda s: (down(s) % nf, 0)),
                  pl.BlockSpec((1, d), lambda s: (0, 0)),
                  pl.BlockSpec((1, d), lambda s: (0, 0))],
        out_specs=pl.BlockSpec((tm, d), lambda s: (down(s) // nf, 0)),
        out_shape=jax.ShapeDtypeStruct((t, d), F32),
        scratch_shapes=[pltpu.VMEM((tm + CONV_HALO, fc), F32),
                        pltpu.VMEM((tm + CONV_HALO, fc), F32),
                        pltpu.VMEM((nf, 2, CONV_HALO, fc), F32),
                        pltpu.VMEM((2, tm, fc), BF16)],
        compiler_params=pltpu.CompilerParams(
            dimension_semantics=("arbitrary",),
            vmem_limit_bytes=_vmem_limit(blocks + scratch)),
        name="ffn",
    )(x1_bf, x1, w_up_bf, w_up_bf, conv_w, conv_w, conv_b, conv_b, w_down_bf, ln_g, ln_b)


def kernel(x, w_in, pool_w, pool_b, pool_scale, hgrn_lb_logits, hgrn_g_norm, w_out,
           ln1_g, ln1_b, w_up, conv_w, conv_b, w_down, ln2_g, ln2_b):
    bsz, seq, d = x.shape
    assert (d, w_in.shape[0], hgrn_lb_logits.shape[0]) == (D_MODEL, DEPTH, DEPTH + 1)
    t = bsz * seq
    x2d = x.reshape(t, d)

    proj = _proj(x2d, w_in[0])
    proj3 = proj.reshape(bsz, seq, IN_COLS)
    a_out = _pool(proj3, pool_w[0].astype(BF16), pool_b[0].reshape(1, POOL_WIDTH),
                  pool_scale[0].reshape(1, POOL_WIDTH))
    b_out, (w_up_bf, w_down_bf, w_out_bf) = _hgrn(
        proj3, hgrn_lb_logits, hgrn_g_norm[0].reshape(1, HGRN_WIDTH),
        (w_up[0], w_down[0], w_out[0]))
    x1, x1_bf = _mix(a_out, b_out, x2d, w_out_bf, ln1_g[0].reshape(1, d), ln1_b[0].reshape(1, d))
    out = _ffn(x1, x1_bf, w_up_bf, conv_w[0], conv_b[0].reshape(1, 2 * D_FF),
               w_down_bf, ln2_g[0].reshape(1, d), ln2_b[0].reshape(1, d), seq)
    return out.reshape(bsz, seq, d)
```

```python
import functools

import jax
import jax.numpy as jnp
from jax import lax
from jax.experimental import pallas as pl
from jax.experimental.pallas import tpu as pltpu

D_MODEL = 2048
POOL_WIDTH = 1024
POOL_WINDOWS = (2, 4, 8, 16)
POOL_GROUP_DIM = 256
HGRN_WIDTH = 1024
HGRN_HEADS = 8
HGRN_DK = 128
IN_COLS = 5120
D_FF = 5632
CONV_WIDTH = 3
DEPTH = 1
ALPHA = (2.0 * DEPTH) ** 0.25
LN_EPS = 1e-5
RMS_EPS = 1e-6

V7X_LANES = 128
V7X_SUBLANES = 8
V7X_VMEM_BYTES = 64 * 1024 * 1024

HGRN_BLOCK = 16
POOL_HALO = 16
CONV_HALO = V7X_SUBLANES
FFN_RES_COLS = 256

BF16 = jnp.bfloat16
F32 = jnp.float32


def _vmem_limit(block_bytes):
    return int(min(V7X_VMEM_BYTES * 7 // 8, 2 * block_bytes + 16 * 1024 * 1024))


def _dot(a, b):
    return jnp.dot(a, b, preferred_element_type=F32)


def _sigmoid(x):
    return 1.0 / (1.0 + jnp.exp(-x))


def _layer_norm(y, g, b):
    mu = jnp.mean(y, axis=-1, keepdims=True)
    yc = y - mu
    var = jnp.mean(yc * yc, axis=-1, keepdims=True)
    return yc * lax.rsqrt(var + LN_EPS) * g + b


def _proj_kernel(x_ref, w_ref, o_ref, wb_ref):
    @pl.when(pl.program_id(1) == 0)
    def _():
        wb_ref[...] = w_ref[...].astype(BF16)

    o_ref[...] = _dot(x_ref[...].astype(BF16), wb_ref[...])


def _proj(x2d, w, tm=1024, tn=1024):
    t, d = x2d.shape
    n = w.shape[1]
    blocks = tm * d * 4 + d * tn * 4 + tm * tn * 4
    return pl.pallas_call(
        _proj_kernel,
        grid=(n // tn, t // tm),
        in_specs=[pl.BlockSpec((tm, d), lambda j, i: (i, 0)),
                  pl.BlockSpec((d, tn), lambda j, i: (0, j))],
        out_specs=pl.BlockSpec((tm, tn), lambda j, i: (i, j)),
        out_shape=jax.ShapeDtypeStruct((t, n), F32),
        scratch_shapes=[pltpu.VMEM((d, tn), BF16)],
        compiler_params=pltpu.CompilerParams(
            dimension_semantics=("arbitrary", "arbitrary"),
            vmem_limit_bytes=_vmem_limit(blocks + d * tn * 2)),
        name="proj",
    )(x2d, w)


def _pool_kernel(u_ref, w_ref, pb_ref, ps_ref, o_ref, buf_ref, *, ts):
    i = pl.program_id(1)

    @pl.when(i == 0)
    def _():
        buf_ref[0:POOL_HALO, :] = jnp.zeros((POOL_HALO, POOL_WIDTH), F32)

    @pl.when(i > 0)
    def _():
        buf_ref[0:POOL_HALO, :] = buf_ref[ts:ts + POOL_HALO, :]

    buf_ref[POOL_HALO:POOL_HALO + ts, :] = u_ref[0]

    pos = i * ts + lax.broadcasted_iota(jnp.int32, (ts, POOL_GROUP_DIM), 0)
    for gi, w in enumerate(POOL_WINDOWS):
        cols = slice(gi * POOL_GROUP_DIM, (gi + 1) * POOL_GROUP_DIM)
        cur = buf_ref[POOL_HALO:POOL_HALO + ts, cols]
        win = cur
        for j in range(1, w):
            win = win + buf_ref[POOL_HALO - j:POOL_HALO - j + ts, cols]
        cnt = jnp.minimum(pos + 1, w).astype(F32)
        pooled = win / cnt - cur
        y = _dot(pooled.astype(BF16), w_ref[gi]) + pb_ref[:, cols]
        o_ref[:, cols] = (y * ps_ref[:, cols]).astype(BF16)


def _pool(proj3, pool_w_bf, pool_b, pool_scale, ts=512):
    b, s, _ = proj3.shape
    nt = s // ts
    blocks = ts * POOL_WIDTH * 4 + 4 * 256 * 256 * 2 + ts * POOL_WIDTH * 2
    return pl.pallas_call(
        functools.partial(_pool_kernel, ts=ts),
        grid=(b, nt),
        in_specs=[pl.BlockSpec((1, ts, POOL_WIDTH), lambda bi, i: (bi, i, 0)),
                  pl.BlockSpec((4, 256, 256), lambda bi, i: (0, 0, 0)),
                  pl.BlockSpec((1, POOL_WIDTH), lambda bi, i: (0, 0)),
                  pl.BlockSpec((1, POOL_WIDTH), lambda bi, i: (0, 0))],
        out_specs=pl.BlockSpec((ts, POOL_WIDTH), lambda bi, i: (bi * nt + i, 0)),
        out_shape=jax.ShapeDtypeStruct((b * s, POOL_WIDTH), BF16),
        scratch_shapes=[pltpu.VMEM((ts + POOL_HALO, POOL_WIDTH), F32)],
        compiler_params=pltpu.CompilerParams(
            dimension_semantics=("arbitrary", "arbitrary"),
            vmem_limit_bytes=_vmem_limit(blocks + (ts + POOL_HALO) * POOL_WIDTH * 4)),
        name="pool",
    )(proj3, pool_w_bf, pool_b, pool_scale)


def _hgrn_kernel(q_ref, z_ref, v_ref, gate_ref, lbl_ref, gn_ref, wu_ref, wd_ref, wo_ref,
                 o_ref, wub_ref, wdb_ref, wob_ref, st_ref, *, ts):
    nb = HGRN_BLOCK
    half = V7X_SUBLANES
    dk = HGRN_DK
    nh = HGRN_HEADS
    width = HGRN_WIDTH

    @pl.when(pl.program_id(0) == 0)
    def _():
        st_ref[...] = jnp.zeros(st_ref.shape, F32)

    wub_ref[...] = wu_ref[...].astype(BF16)
    wdb_ref[...] = wd_ref[...].astype(BF16)
    wob_ref[...] = wo_ref[...].astype(BF16)

    lg = lbl_ref[...]
    mx = jnp.max(lg, axis=0, keepdims=True)
    ex = jnp.exp(lg - mx)
    lb = ex[0:1, :] / jnp.sum(ex, axis=0, keepdims=True)
    gn = gn_ref[...]

    rowi = lax.broadcasted_iota(jnp.int32, (half, width), 0)
    sel = (lax.broadcasted_iota(jnp.int32, (nb, nb * dk), 0)
           == lax.broadcasted_iota(jnp.int32, (nb, nb * dk), 1) // dk).astype(BF16)
    lane_head = lax.broadcasted_iota(jnp.int32, (nb, nh * nb), 1) // nb

    def block_cumsum(x):
        lo, hi = x[0:half, :], x[half:nb, :]
        sh = 1
        while sh < half:
            r_lo = pltpu.roll(lo, sh, axis=0)
            r_hi = pltpu.roll(hi, sh, axis=0)
            lo, hi = (lo + jnp.where(rowi >= sh, r_lo, 0.0),
                      hi + jnp.where(rowi >= sh, r_hi, r_lo))
            sh *= 2
        return jnp.concatenate([lo, hi + lo], axis=0)

    def front(n, bi):
        r0 = pl.multiple_of(n * nb, nb)
        q = q_ref[bi, pl.ds(r0, nb), :]
        z = z_ref[bi, pl.ds(r0, nb), :]

        f = lb + (1.0 - lb) * _sigmoid(z)
        kk = 1.0 - f
        k_lo, k_hi = kk[0:half, :], kk[half:nb, :]
        b = block_cumsum(jnp.log(f))
        a = (q * jnp.exp(b)).astype(BF16)
        dec = jnp.exp(b[nb - 1:nb, :])

        u_lo = None
        u_hi = None
        cols = []
        for t in range(nb):
            ft = jnp.broadcast_to(f[t:t + 1, :], (half, width))
            qt = jnp.broadcast_to(q[t:t + 1, :], (half, width))
            if t == 0:
                u_lo = jnp.where(rowi == 0, 1.0, 0.0)
            elif t < half:
                u_lo = jnp.where(rowi == t, 1.0, u_lo * ft)
            else:
                u_lo = u_lo * ft
                u_hi = jnp.where(rowi == t - half, 1.0, u_hi * ft if t > half else 0.0)
            w_lo = u_lo * k_lo
            w_hi = u_hi * k_hi if t >= half else jnp.zeros((half, width), F32)
            cols.append(jnp.concatenate([w_lo * qt, w_hi * qt], axis=0).astype(BF16))
        kx = jnp.concatenate([w_lo, w_hi], axis=0).astype(BF16)

        pcat = jnp.concatenate(
            [jnp.concatenate([c[:, h * dk:(h + 1) * dk] for c in cols], axis=1)
             for h in range(nh)], axis=0)
        sc = lax.dot_general(sel, pcat, (((1,), (1,)), ((), ())),
                             preferred_element_type=F32)
        return sc, a, kx, dec

    def back(n, bi, sc, a, kx, dec):
        r0 = pl.multiple_of(n * nb, nb)
        v = v_ref[bi, pl.ds(r0, nb), :]
        gate = gate_ref[bi, pl.ds(r0, nb), :]
        sc_heads = jnp.concatenate(
            [jnp.where(lane_head == h, sc, 0.0) for h in range(nh)], axis=0).astype(BF16)
        v_rows = jnp.concatenate([v[:, h * dk:(h + 1) * dk] for h in range(nh)],
                                 axis=0).astype(BF16)
        o_intra = _dot(sc_heads, v_rows)

        outs = []
        for h in range(nh):
            hs = slice(h * dk, (h + 1) * dk)
            st = st_ref[bi, h]
            o_h = lax.dot_general(a[:, hs], st.astype(BF16), (((1,), (1,)), ((), ())),
                                  preferred_element_type=F32)
            o_h = o_h + o_intra[h * nb:(h + 1) * nb, :]
            ms = jnp.mean(o_h * o_h, axis=-1, keepdims=True)
            outs.append(o_h * lax.rsqrt(ms + RMS_EPS))
            upd = _dot(v[:, hs].T.astype(BF16), kx[:, hs])
            st_ref[bi, h] = st * dec[:, hs] + upd

        o = jnp.concatenate(outs, axis=1) * gn
        o = o * (gate * _sigmoid(gate))
        o_ref[bi, pl.ds(r0, nb), :] = o.astype(BF16)

    seqs = range(q_ref.shape[0])

    def step(n, carry):
        nxt = tuple(front(n + 1, bi) for bi in seqs)
        for bi in seqs:
            back(n, bi, *carry[bi])
        return nxt

    last = lax.fori_loop(0, ts // nb - 1, step, tuple(front(0, bi) for bi in seqs), unroll=5)
    for bi in seqs:
        back(ts // nb - 1, bi, *last[bi])


def _hgrn(proj3, lb_logits, g_norm, side_weights, ts=256):
    b, s, _ = proj3.shape
    nt = s // ts
    w = HGRN_WIDTH

    def col(c):
        return pl.BlockSpec((b, ts, w), lambda i: (0, i, c))

    def slab(wt):
        assert wt.shape[0] % (nt * 2 * V7X_SUBLANES) == 0
        return pl.BlockSpec((wt.shape[0] // nt, wt.shape[1]), lambda i: (i, 0))

    side_elems = sum(wt.size for wt in side_weights) // nt
    blocks = b * (4 * ts * w * 4 + ts * w * 2) + side_elems * (4 + 2)
    out, *side_bf = pl.pallas_call(
        functools.partial(_hgrn_kernel, ts=ts),
        grid=(nt,),
        in_specs=[col(1), col(2), col(3), col(4),
                  pl.BlockSpec((2, w), lambda i: (0, 0)),
                  pl.BlockSpec((1, w), lambda i: (0, 0))] + [slab(wt) for wt in side_weights],
        out_specs=[pl.BlockSpec((b, ts, w), lambda i: (0, i, 0))]
        + [slab(wt) for wt in side_weights],
        out_shape=[jax.ShapeDtypeStruct((b, s, w), BF16)]
        + [jax.ShapeDtypeStruct(wt.shape, BF16) for wt in side_weights],
        scratch_shapes=[pltpu.VMEM((b, HGRN_HEADS, HGRN_DK, HGRN_DK), F32)],
        compiler_params=pltpu.CompilerParams(
            dimension_semantics=("arbitrary",),
            vmem_limit_bytes=_vmem_limit(blocks)),
        name="hgrn",
    )(proj3, proj3, proj3, proj3, lb_logits, g_norm, *side_weights)
    return out.reshape(b * s, w), side_bf


def _mix_kernel(a_ref, b_ref, x_ref, wt_ref, wb_ref, g_ref, bt_ref, o_ref, ob_ref):
    mix = _dot(a_ref[...], wt_ref[...]) + _dot(b_ref[...], wb_ref[...])
    y = ALPHA * x_ref[...] + mix
    x1 = _layer_norm(y, g_ref[...], bt_ref[...])
    o_ref[...] = x1
    ob_ref[...] = x1.astype(BF16)


def _mix(a_out, b_out, x2d, w_out_bf, ln_g, ln_b, tm=512):
    t, d = x2d.shape
    hw = a_out.shape[1]
    blocks = 2 * tm * hw * 2 + 2 * tm * d * 4 + tm * d * 2 + d * d * 2
    return pl.pallas_call(
        _mix_kernel,
        grid=(t // tm,),
        in_specs=[pl.BlockSpec((tm, hw), lambda i: (i, 0)),
                  pl.BlockSpec((tm, hw), lambda i: (i, 0)),
                  pl.BlockSpec((tm, d), lambda i: (i, 0)),
                  pl.BlockSpec((hw, d), lambda i: (0, 0)),
                  pl.BlockSpec((hw, d), lambda i: (1, 0)),
                  pl.BlockSpec((1, d), lambda i: (0, 0)),
                  pl.BlockSpec((1, d), lambda i: (0, 0))],
        out_specs=[pl.BlockSpec((tm, d), lambda i: (i, 0)),
                   pl.BlockSpec((tm, d), lambda i: (i, 0))],
        out_shape=[jax.ShapeDtypeStruct((t, d), F32), jax.ShapeDtypeStruct((t, d), BF16)],
        compiler_params=pltpu.CompilerParams(
            dimension_semantics=("arbitrary",),
            vmem_limit_bytes=_vmem_limit(blocks)),
        name="mix",
    )(a_out, b_out, x2d, w_out_bf, w_out_bf, ln_g, ln_b)


def _ffn_kernel(xb_ref, xres_ref, wg_ref, wv_ref, cwg_ref, cwv_ref, cbg_ref, cbv_ref, wd_ref,
                g_ref, bt_ref, o_ref, ug_ref, uv_ref, tail_ref, h_ref,
                *, tm, tiles_per_seq, nf, ns):
    s = pl.program_id(0)
    su = jnp.minimum(s, ns - 1)
    iu = su // nf
    ju = su % nf
    jd = jnp.maximum(s - 1, 0) % nf
    halo = CONV_HALO

    @pl.when(s == 0)
    def _():
        tail_ref[...] = jnp.zeros(tail_ref.shape, F32)
        h_ref[...] = jnp.zeros(h_ref.shape, BF16)

    @pl.when(jd == 0)
    def _():
        o_ref[...] = jnp.zeros(o_ref.shape, F32)

    seq_start = (iu % tiles_per_seq) == 0

    def conv(w_ref, cw_ref, cb_ref, ubuf_ref, slot):
        u = _dot(xb_ref[...], w_ref[...])
        ubuf_ref[0:halo, :] = jnp.where(seq_start, 0.0, tail_ref[ju, slot])
        ubuf_ref[halo:halo + tm, :] = u
        u1 = ubuf_ref[halo - 1:halo - 1 + tm, :]
        u2 = ubuf_ref[halo - 2:halo - 2 + tm, :]
        tail_ref[ju, slot] = ubuf_ref[tm:tm + halo, :]
        cw = cw_ref[...]
        return cb_ref[...] + cw[0:1, :] * u2 + cw[1:2, :] * u1 + cw[2:3, :] * u

    def body(h_prev_ref, h_next_ref):
        gc = conv(wg_ref, cwg_ref, cbg_ref, ug_ref, 0)
        vc = conv(wv_ref, cwv_ref, cbv_ref, uv_ref, 1)
        o_ref[...] += _dot(h_prev_ref[...], wd_ref[...])
        h_next_ref[...] = ((gc * _sigmoid(gc)) * vc).astype(BF16)

    @pl.when(s % 2 == 0)
    def _():
        body(h_ref.at[1], h_ref.at[0])

    @pl.when(s % 2 == 1)
    def _():
        body(h_ref.at[0], h_ref.at[1])

    for c in range(o_ref.shape[1] // FFN_RES_COLS):
        @pl.when(jd == c)
        def _():
            cols = slice(c * FFN_RES_COLS, (c + 1) * FFN_RES_COLS)
            o_ref[:, cols] += ALPHA * xres_ref[...]

    @pl.when(jnp.logical_and(jd == nf - 1, s > 0))
    def _():
        o_ref[...] = _layer_norm(o_ref[...], g_ref[...], bt_ref[...])


def _ffn(x1, x1_bf, w_up_bf, conv_w, conv_b, w_down_bf, ln_g, ln_b, seq, tm=1024, fc=512):
    t, d = x1.shape
    nf = D_FF // fc
    ns = (t // tm) * nf
    n_res = d // FFN_RES_COLS
    assert n_res <= nf

    def up(s):
        return jnp.minimum(s, ns - 1)

    def down(s):
        return jnp.maximum(s - 1, 0)

    blocks = (tm * d * 2 + tm * FFN_RES_COLS * 4 + tm * d * 4 + 2 * d * fc * 2 + fc * d * 2
              + 8 * fc * 4)
    scratch = (2 * (tm + CONV_HALO) * fc * 4 + nf * 2 * CONV_HALO * fc * 4 + 2 * tm * fc * 2)
    return pl.pallas_call(
        functools.partial(_ffn_kernel, tm=tm, tiles_per_seq=seq // tm, nf=nf, ns=ns),
        grid=(ns + 1,),
        in_specs=[pl.BlockSpec((tm, d), lambda s: (up(s) // nf, 0)),
                  pl.BlockSpec((tm, FFN_RES_COLS),
                               lambda s: (down(s) // nf, jnp.minimum(down(s) % nf, n_res - 1))),
                  pl.BlockSpec((d, fc), lambda s: (0, up(s) % nf)),
                  pl.BlockSpec((d, fc), lambda s: (0, up(s) % nf + nf)),
                  pl.BlockSpec((CONV_WIDTH, fc), lambda s: (0, up(s) % nf)),
                  pl.BlockSpec((CONV_WIDTH, fc), lambda s: (0, up(s) % nf + nf)),
                  pl.BlockSpec((1, fc), lambda s: (0, up(s) % nf)),
                  pl.BlockSpec((1, fc), lambda s: (0, up(s) % nf + nf)),
                  pl.BlockSpec((fc, d), lambda s: (down(s) % nf, 0)),
                  pl.BlockSpec((1, d), lambda s: (0, 0)),
                  pl.BlockSpec((1, d), lambda s: (0, 0))],
        out_specs=pl.BlockSpec((tm, d), lambda s: (down(s) // nf, 0)),
        out_shape=jax.ShapeDtypeStruct((t, d), F32),
        scratch_shapes=[pltpu.VMEM((tm + CONV_HALO, fc), F32),
                        pltpu.VMEM((tm + CONV_HALO, fc), F32),
                        pltpu.VMEM((nf, 2, CONV_HALO, fc), F32),
                        pltpu.VMEM((2, tm, fc), BF16)],
        compiler_params=pltpu.CompilerParams(
            dimension_semantics=("arbitrary",),
            vmem_limit_bytes=_vmem_limit(blocks + scratch)),
        name="ffn",
    )(x1_bf, x1, w_up_bf, w_up_bf, conv_w, conv_w, conv_b, conv_b, w_down_bf, ln_g, ln_b)


def kernel(x, w_in, pool_w, pool_b, pool_scale, hgrn_lb_logits, hgrn_g_norm, w_out,
           ln1_g, ln1_b, w_up, conv_w, conv_b, w_down, ln2_g, ln2_b):
    bsz, seq, d = x.shape
    assert (d, w_in.shape[0], hgrn_lb_logits.shape[0]) == (D_MODEL, DEPTH, DEPTH + 1)
    t = bsz * seq
    x2d = x.reshape(t, d)

    proj = _proj(x2d, w_in[0])
    proj3 = proj.reshape(bsz, seq, IN_COLS)
    a_out = _pool(proj3, pool_w[0].astype(BF16), pool_b[0].reshape(1, POOL_WIDTH),
                  pool_scale[0].reshape(1, POOL_WIDTH))
    b_out, (w_up_bf, w_down_bf, w_out_bf) = _hgrn(
        proj3, hgrn_lb_logits, hgrn_g_norm[0].reshape(1, HGRN_WIDTH),
        (w_up[0], w_down[0], w_out[0]))
    x1, x1_bf = _mix(a_out, b_out, x2d, w_out_bf, ln1_g[0].reshape(1, d), ln1_b[0].reshape(1, d))
    out = _ffn(x1, x1_bf, w_up_bf, conv_w[0], conv_b[0].reshape(1, 2 * D_FF),
               w_down_bf, ln2_g[0].reshape(1, d), ln2_b[0].reshape(1, d), seq)
    return out.reshape(bsz, seq, d)
```

```python
import functools

import jax
import jax.numpy as jnp
from jax import lax
from jax.experimental import pallas as pl
from jax.experimental.pallas import tpu as pltpu

D_MODEL = 2048
POOL_WIDTH = 1024
POOL_WINDOWS = (2, 4, 8, 16)
POOL_GROUP_DIM = 256
HGRN_WIDTH = 1024
HGRN_HEADS = 8
HGRN_DK = 128
IN_COLS = 5120
D_FF = 5632
CONV_WIDTH = 3
DEPTH = 1
ALPHA = (2.0 * DEPTH) ** 0.25
LN_EPS = 1e-5
RMS_EPS = 1e-6

V7X_LANES = 128
V7X_SUBLANES = 8
V7X_VMEM_BYTES = 64 * 1024 * 1024

HGRN_BLOCK = 16
POOL_HALO = 16
CONV_HALO = V7X_SUBLANES
FFN_RES_COLS = 256

BF16 = jnp.bfloat16
F32 = jnp.float32


def _vmem_limit(block_bytes):
    return int(min(V7X_VMEM_BYTES * 7 // 8, 2 * block_bytes + 16 * 1024 * 1024))


def _dot(a, b):
    return jnp.dot(a, b, preferred_element_type=F32)


def _sigmoid(x):
    return 1.0 / (1.0 + jnp.exp(-x))


def _layer_norm(y, g, b):
    mu = jnp.mean(y, axis=-1, keepdims=True)
    yc = y - mu
    var = jnp.mean(yc * yc, axis=-1, keepdims=True)
    return yc * lax.rsqrt(var + LN_EPS) * g + b


def _proj_kernel(x_ref, w_ref, o_ref, wb_ref):
    @pl.when(pl.program_id(1) == 0)
    def _():
        wb_ref[...] = w_ref[...].astype(BF16)

    o_ref[...] = _dot(x_ref[...].astype(BF16), wb_ref[...])


def _proj(x2d, w, tm=1024, tn=1024):
    t, d = x2d.shape
    n = w.shape[1]
    blocks = tm * d * 4 + d * tn * 4 + tm * tn * 4
    return pl.pallas_call(
        _proj_kernel,
        grid=(n // tn, t // tm),
        in_specs=[pl.BlockSpec((tm, d), lambda j, i: (i, 0)),
                  pl.BlockSpec((d, tn), lambda j, i: (0, j))],
        out_specs=pl.BlockSpec((tm, tn), lambda j, i: (i, j)),
        out_shape=jax.ShapeDtypeStruct((t, n), F32),
        scratch_shapes=[pltpu.VMEM((d, tn), BF16)],
        compiler_params=pltpu.CompilerParams(
            dimension_semantics=("arbitrary", "arbitrary"),
            vmem_limit_bytes=_vmem_limit(blocks + d * tn * 2)),
        name="proj",
    )(x2d, w)


def _pool_kernel(u_ref, w_ref, pb_ref, ps_ref, o_ref, buf_ref, *, ts):
    i = pl.program_id(1)

    @pl.when(i == 0)
    def _():
        buf_ref[0:POOL_HALO, :] = jnp.zeros((POOL_HALO, POOL_WIDTH), F32)

    @pl.when(i > 0)
    def _():
        buf_ref[0:POOL_HALO, :] = buf_ref[ts:ts + POOL_HALO, :]

    buf_ref[POOL_HALO:POOL_HALO + ts, :] = u_ref[0]

    pos = i * ts + lax.broadcasted_iota(jnp.int32, (ts, POOL_GROUP_DIM), 0)
    for gi, w in enumerate(POOL_WINDOWS):
        cols = slice(gi * POOL_GROUP_DIM, (gi + 1) * POOL_GROUP_DIM)
        cur = buf_ref[POOL_HALO:POOL_HALO + ts, cols]
        win = cur
        for j in range(1, w):
            win = win + buf_ref[POOL_HALO - j:POOL_HALO - j + ts, cols]
        cnt = jnp.minimum(pos + 1, w).astype(F32)
        pooled = win / cnt - cur
        y = _dot(pooled.astype(BF16), w_ref[gi]) + pb_ref[:, cols]
        o_ref[:, cols] = (y * ps_ref[:, cols]).astype(BF16)


def _pool(proj3, pool_w_bf, pool_b, pool_scale, ts=512):
    b, s, _ = proj3.shape
    nt = s // ts
    blocks = ts * POOL_WIDTH * 4 + 4 * 256 * 256 * 2 + ts * POOL_WIDTH * 2
    return pl.pallas_call(
        functools.partial(_pool_kernel, ts=ts),
        grid=(b, nt),
        in_specs=[pl.BlockSpec((1, ts, POOL_WIDTH), lambda bi, i: (bi, i, 0)),
                  pl.BlockSpec((4, 256, 256), lambda bi, i: (0, 0, 0)),
                  pl.BlockSpec((1, POOL_WIDTH), lambda bi, i: (0, 0)),
                  pl.BlockSpec((1, POOL_WIDTH), lambda bi, i: (0, 0))],
        out_specs=pl.BlockSpec((ts, POOL_WIDTH), lambda bi, i: (bi * nt + i, 0)),
        out_shape=jax.ShapeDtypeStruct((b * s, POOL_WIDTH), BF16),
        scratch_shapes=[pltpu.VMEM((ts + POOL_HALO, POOL_WIDTH), F32)],
        compiler_params=pltpu.CompilerParams(
            dimension_semantics=("arbitrary", "arbitrary"),
            vmem_limit_bytes=_vmem_limit(blocks + (ts + POOL_HALO) * POOL_WIDTH * 4)),
        name="pool",
    )(proj3, pool_w_bf, pool_b, pool_scale)


def _hgrn_kernel(q_ref, z_ref, v_ref, gate_ref, lbl_ref, gn_ref, wu_ref, wd_ref, wo_ref,
                 o_ref, wub_ref, wdb_ref, wob_ref, st_ref, *, ts):
    nb = HGRN_BLOCK
    half = V7X_SUBLANES
    dk = HGRN_DK
    nh = HGRN_HEADS
    width = HGRN_WIDTH

    @pl.when(pl.program_id(0) == 0)
    def _():
        st_ref[...] = jnp.zeros(st_ref.shape, F32)

    wub_ref[...] = wu_ref[...].astype(BF16)
    wdb_ref[...] = wd_ref[...].astype(BF16)
    wob_ref[...] = wo_ref[...].astype(BF16)

    lg = lbl_ref[...]
    mx = jnp.max(lg, axis=0, keepdims=True)
    ex = jnp.exp(lg - mx)
    lb = ex[0:1, :] / jnp.sum(ex, axis=0, keepdims=True)
    gn = gn_ref[...]

    rowi = lax.broadcasted_iota(jnp.int32, (half, width), 0)
    sel = (lax.broadcasted_iota(jnp.int32, (nb, nb * dk), 0)
           == lax.broadcasted_iota(jnp.int32, (nb, nb * dk), 1) // dk).astype(BF16)
    lane_head = lax.broadcasted_iota(jnp.int32, (nb, nh * nb), 1) // nb

    def block_cumsum(x):
        lo, hi = x[0:half, :], x[half:nb, :]
        sh = 1
        while sh < half:
            r_lo = pltpu.roll(lo, sh, axis=0)
            r_hi = pltpu.roll(hi, sh, axis=0)
            lo, hi = (lo + jnp.where(rowi >= sh, r_lo, 0.0),
                      hi + jnp.where(rowi >= sh, r_hi, r_lo))
            sh *= 2
        return jnp.concatenate([lo, hi + lo], axis=0)

    def front(n, bi):
        r0 = n * nb
        q = q_ref[bi, pl.ds(r0, nb), :]
        z = z_ref[bi, pl.ds(r0, nb), :]

        f = lb + (1.0 - lb) * _sigmoid(z)
        kk = 1.0 - f
        k_lo, k_hi = kk[0:half, :], kk[half:nb, :]
        b = block_cumsum(jnp.log(f))
        a = (q * jnp.exp(b)).astype(BF16)
        dec = jnp.exp(b[nb - 1:nb, :])

        u_lo = None
        u_hi = None
        cols = []
        for t in range(nb):
            ft = jnp.broadcast_to(f[t:t + 1, :], (half, width))
            qt = jnp.broadcast_to(q[t:t + 1, :], (half, width))
            if t == 0:
                u_lo = jnp.where(rowi == 0, 1.0, 0.0)
            elif t < half:
                u_lo = jnp.where(rowi == t, 1.0, u_lo * ft)
            else:
                u_lo = u_lo * ft
                u_hi = jnp.where(rowi == t - half, 1.0, u_hi * ft if t > half else 0.0)
            w_lo = u_lo * k_lo
            w_hi = u_hi * k_hi if t >= half else jnp.zeros((half, width), F32)
            cols.append(jnp.concatenate([w_lo * qt, w_hi * qt], axis=0).astype(BF16))
        kx = jnp.concatenate([w_lo, w_hi], axis=0).astype(BF16)

        pcat = jnp.concatenate(
            [jnp.concatenate([c[:, h * dk:(h + 1) * dk] for c in cols], axis=1)
             for h in range(nh)], axis=0)
        sc = lax.dot_general(sel, pcat, (((1,), (1,)), ((), ())),
                             preferred_element_type=F32)
        return sc, a, kx, dec

    def back(n, bi, sc, a, kx, dec):
        r0 = n * nb
        v = v_ref[bi, pl.ds(r0, nb), :]
        gate = gate_ref[bi, pl.ds(r0, nb), :]
        sc_heads = jnp.concatenate(
            [jnp.where(lane_head == h, sc, 0.0) for h in range(nh)], axis=0).astype(BF16)
        v_rows = jnp.concatenate([v[:, h * dk:(h + 1) * dk] for h in range(nh)],
                                 axis=0).astype(BF16)
        o_intra = _dot(sc_heads, v_rows)

        outs = []
        for h in range(nh):
            hs = slice(h * dk, (h + 1) * dk)
            st = st_ref[bi, h]
            o_h = lax.dot_general(a[:, hs], st.astype(BF16), (((1,), (1,)), ((), ())),
                                  preferred_element_type=F32)
            o_h = o_h + o_intra[h * nb:(h + 1) * nb, :]
            ms = jnp.mean(o_h * o_h, axis=-1, keepdims=True)
            outs.append(o_h * lax.rsqrt(ms + RMS_EPS))
            upd = _dot(v[:, hs].T.astype(BF16), kx[:, hs])
            st_ref[bi, h] = st * dec[:, hs] + upd

        o = jnp.concatenate(outs, axis=1) * gn
        o = o * (gate * _sigmoid(gate))
        o_ref[bi, pl.ds(r0, nb), :] = o.astype(BF16)

    seqs = range(q_ref.shape[0])

    def step(n, carry):
        nxt = tuple(front(n + 1, bi) for bi in seqs)
        for bi in seqs:
            back(n, bi, *carry[bi])
        return nxt

    carry = tuple(front(0, bi) for bi in seqs)
    for n in range(ts // nb - 1):
        carry = step(n, carry)
    for bi in seqs:
        back(ts // nb - 1, bi, *carry[bi])


def _hgrn(proj3, lb_logits, g_norm, side_weights, ts=256):
    b, s, _ = proj3.shape
    nt = s // ts
    w = HGRN_WIDTH

    def col(c):
        return pl.BlockSpec((b, ts, w), lambda i: (0, i, c))

    def slab(wt):
        assert wt.shape[0] % (nt * 2 * V7X_SUBLANES) == 0
        return pl.BlockSpec((wt.shape[0] // nt, wt.shape[1]), lambda i: (i, 0))

    side_elems = sum(wt.size for wt in side_weights) // nt
    blocks = b * (4 * ts * w * 4 + ts * w * 2) + side_elems * (4 + 2)
    out, *side_bf = pl.pallas_call(
        functools.partial(_hgrn_kernel, ts=ts),
        grid=(nt,),
        in_specs=[col(1), col(2), col(3), col(4),
                  pl.BlockSpec((2, w), lambda i: (0, 0)),
                  pl.BlockSpec((1, w), lambda i: (0, 0))] + [slab(wt) for wt in side_weights],
        out_specs=[pl.BlockSpec((b, ts, w), lambda i: (0, i, 0))]
        + [slab(wt) for wt in side_weights],
        out_shape=[jax.ShapeDtypeStruct((b, s, w), BF16)]
        + [jax.ShapeDtypeStruct(wt.shape, BF16) for wt in side_weights],
        scratch_shapes=[pltpu.VMEM((b, HGRN_HEADS, HGRN_DK, HGRN_DK), F32)],
        compiler_params=pltpu.CompilerParams(
            dimension_semantics=("arbitrary",),
            vmem_limit_bytes=_vmem_limit(blocks)),
        name="hgrn",
    )(proj3, proj3, proj3, proj3, lb_logits, g_norm, *side_weights)
    return out.reshape(b * s, w), side_bf


def _mix_kernel(a_ref, b_ref, x_ref, wt_ref, wb_ref, g_ref, bt_ref, o_ref, ob_ref):
    mix = _dot(a_ref[...], wt_ref[...]) + _dot(b_ref[...], wb_ref[...])
    y = ALPHA * x_ref[...] + mix
    x1 = _layer_norm(y, g_ref[...], bt_ref[...])
    o_ref[...] = x1
    ob_ref[...] = x1.astype(BF16)


def _mix(a_out, b_out, x2d, w_out_bf, ln_g, ln_b, tm=512):
    t, d = x2d.shape
    hw = a_out.shape[1]
    blocks = 2 * tm * hw * 2 + 2 * tm * d * 4 + tm * d * 2 + d * d * 2
    return pl.pallas_call(
        _mix_kernel,
        grid=(t // tm,),
        in_specs=[pl.BlockSpec((tm, hw), lambda i: (i, 0)),
                  pl.BlockSpec((tm, hw), lambda i: (i, 0)),
                  pl.BlockSpec((tm, d), lambda i: (i, 0)),
                  pl.BlockSpec((hw, d), lambda i: (0, 0)),
                  pl.BlockSpec((hw, d), lambda i: (1, 0)),
                  pl.BlockSpec((1, d), lambda i: (0, 0)),
                  pl.BlockSpec((1, d), lambda i: (0, 0))],
        out_specs=[pl.BlockSpec((tm, d), lambda i: (i, 0)),
                   pl.BlockSpec((tm, d), lambda i: (i, 0))],
        out_shape=[jax.ShapeDtypeStruct((t, d), F32), jax.ShapeDtypeStruct((t, d), BF16)],
        compiler_params=pltpu.CompilerParams(
            dimension_semantics=("arbitrary",),
            vmem_limit_bytes=_vmem_limit(blocks)),
        name="mix",
    )(a_out, b_out, x2d, w_out_bf, w_out_bf, ln_g, ln_b)


def _ffn_kernel(xb_ref, xres_ref, wg_ref, wv_ref, cwg_ref, cwv_ref, cbg_ref, cbv_ref, wd_ref,
                g_ref, bt_ref, o_ref, ug_ref, uv_ref, tail_ref, h_ref,
                *, tm, tiles_per_seq, nf, ns):
    s = pl.program_id(0)
    su = jnp.minimum(s, ns - 1)
    iu = su // nf
    ju = su % nf
    jd = jnp.maximum(s - 1, 0) % nf
    halo = CONV_HALO

    @pl.when(s == 0)
    def _():
        tail_ref[...] = jnp.zeros(tail_ref.shape, F32)
        h_ref[...] = jnp.zeros(h_ref.shape, BF16)

    @pl.when(jd == 0)
    def _():
        o_ref[...] = jnp.zeros(o_ref.shape, F32)

    seq_start = (iu % tiles_per_seq) == 0

    def conv(w_ref, cw_ref, cb_ref, ubuf_ref, slot):
        u = _dot(xb_ref[...], w_ref[...])
        ubuf_ref[0:halo, :] = jnp.where(seq_start, 0.0, tail_ref[ju, slot])
        ubuf_ref[halo:halo + tm, :] = u
        u1 = ubuf_ref[halo - 1:halo - 1 + tm, :]
        u2 = ubuf_ref[halo - 2:halo - 2 + tm, :]
        tail_ref[ju, slot] = ubuf_ref[tm:tm + halo, :]
        cw = cw_ref[...]
        return cb_ref[...] + cw[0:1, :] * u2 + cw[1:2, :] * u1 + cw[2:3, :] * u

    def body(h_prev_ref, h_next_ref):
        gc = conv(wg_ref, cwg_ref, cbg_ref, ug_ref, 0)
        vc = conv(wv_ref, cwv_ref, cbv_ref, uv_ref, 1)
        o_ref[...] += _dot(h_prev_ref[...], wd_ref[...])
        h_next_ref[...] = ((gc * _sigmoid(gc)) * vc).astype(BF16)

    @pl.when(s % 2 == 0)
    def _():
        body(h_ref.at[1], h_ref.at[0])

    @pl.when(s % 2 == 1)
    def _():
        body(h_ref.at[0], h_ref.at[1])

    for c in range(o_ref.shape[1] // FFN_RES_COLS):
        @pl.when(jd == c)
        def _():
            cols = slice(c * FFN_RES_COLS, (c + 1) * FFN_RES_COLS)
            o_ref[:, cols] += ALPHA * xres_ref[...]

    @pl.when(jnp.logical_and(jd == nf - 1, s > 0))
    def _():
        o_ref[...] = _layer_norm(o_ref[...], g_ref[...], bt_ref[...])


def _ffn(x1, x1_bf, w_up_bf, conv_w, conv_b, w_down_bf, ln_g, ln_b, seq, tm=1024, fc=512):
    t, d = x1.shape
    nf = D_FF // fc
    ns = (t // tm) * nf
    n_res = d // FFN_RES_COLS
    assert n_res <= nf

    def up(s):
        return jnp.minimum(s, ns - 1)

    def down(s):
        return jnp.maximum(s - 1, 0)

    blocks = (tm * d * 2 + tm * FFN_RES_COLS * 4 + tm * d * 4 + 2 * d * fc * 2 + fc * d * 2
              + 8 * fc * 4)
    scratch = (2 * (tm + CONV_HALO) * fc * 4 + nf * 2 * CONV_HALO * fc * 4 + 2 * tm * fc * 2)
    return pl.pallas_call(
        functools.partial(_ffn_kernel, tm=tm, tiles_per_seq=seq // tm, nf=nf, ns=ns),
        grid=(ns + 1,),
        in_specs=[pl.BlockSpec((tm, d), lambda s: (up(s) // nf, 0)),
                  pl.BlockSpec((tm, FFN_RES_COLS),
                               lambda s: (down(s) // nf, jnp.minimum(down(s) % nf, n_res - 1))),
                  pl.BlockSpec((d, fc), lambda s: (0, up(s) % nf)),
                  pl.BlockSpec((d, fc), lambda s: (0, up(s) % nf + nf)),
                  pl.BlockSpec((CONV_WIDTH, fc), lambda s: (0, up(s) % nf)),
                  pl.BlockSpec((CONV_WIDTH, fc), lambda s: (0, up(s) % nf + nf)),
                  pl.BlockSpec((1, fc), lambda s: (0, up(s) % nf)),
                  pl.BlockSpec((1, fc), lambda s: (0, up(s) % nf + nf)),
                  pl.BlockSpec((fc, d), lambda s: (down(s) % nf, 0)),
                  pl.BlockSpec((1, d), lambda s: (0, 0)),
                  pl.BlockSpec((1, d), lambda s: (0, 0))],
        out_specs=pl.BlockSpec((tm, d), lambda s: (down(s) // nf, 0)),
        out_shape=jax.ShapeDtypeStruct((t, d), F32),
        scratch_shapes=[pltpu.VMEM((tm + CONV_HALO, fc), F32),
                        pltpu.VMEM((tm + CONV_HALO, fc), F32),
                        pltpu.VMEM((nf, 2, CONV_HALO, fc), F32),
                        pltpu.VMEM((2, tm, fc), BF16)],
        compiler_params=pltpu.CompilerParams(
            dimension_semantics=("arbitrary",),
            vmem_limit_bytes=_vmem_limit(blocks + scratch)),
        name="ffn",
    )(x1_bf, x1, w_up_bf, w_up_bf, conv_w, conv_w, conv_b, conv_b, w_down_bf, ln_g, ln_b)


def kernel(x, w_in, pool_w, pool_b, pool_scale, hgrn_lb_logits, hgrn_g_norm, w_out,
           ln1_g, ln1_b, w_up, conv_w, conv_b, w_down, ln2_g, ln2_b):
    bsz, seq, d = x.shape
    assert (d, w_in.shape[0], hgrn_lb_logits.shape[0]) == (D_MODEL, DEPTH, DEPTH + 1)
    t = bsz * seq
    x2d = x.reshape(t, d)

    proj = _proj(x2d, w_in[0])
    proj3 = proj.reshape(bsz, seq, IN_COLS)
    a_out = _pool(proj3, pool_w[0].astype(BF16), pool_b[0].reshape(1, POOL_WIDTH),
                  pool_scale[0].reshape(1, POOL_WIDTH))
    b_out, (w_up_bf, w_down_bf, w_out_bf) = _hgrn(
        proj3, hgrn_lb_logits, hgrn_g_norm[0].reshape(1, HGRN_WIDTH),
        (w_up[0], w_down[0], w_out[0]))
    x1, x1_bf = _mix(a_out, b_out, x2d, w_out_bf, ln1_g[0].reshape(1, d), ln1_b[0].reshape(1, d))
    out = _ffn(x1, x1_bf, w_up_bf, conv_w[0], conv_b[0].reshape(1, 2 * D_FF),
               w_down_bf, ln2_g[0].reshape(1, d), ln2_b[0].reshape(1, d), seq)
    return out.reshape(bsz, seq, d)
```

```python
import functools

import jax
import jax.numpy as jnp
from jax import lax
from jax.experimental import pallas as pl
from jax.experimental.pallas import tpu as pltpu

D_MODEL = 2048
POOL_WIDTH = 1024
POOL_WINDOWS = (2, 4, 8, 16)
POOL_GROUP_DIM = 256
HGRN_WIDTH = 1024
HGRN_HEADS = 8
HGRN_DK = 128
IN_COLS = 5120
PROJ_F_BLOCK = 2
PROJ_GATE_BLOCK = 4
D_FF = 5632
CONV_WIDTH = 3
DEPTH = 1
ALPHA = (2.0 * DEPTH) ** 0.25
LN_EPS = 1e-5
RMS_EPS = 1e-6

V7X_LANES = 128
V7X_SUBLANES = 8
V7X_VMEM_BYTES = 64 * 1024 * 1024

HGRN_BLOCK = 16
POOL_HALO = 32
CONV_HALO = V7X_SUBLANES
FFN_RES_COLS = 256

BF16 = jnp.bfloat16
F32 = jnp.float32


def _vmem_limit(block_bytes):
    return int(min(V7X_VMEM_BYTES * 7 // 8, 2 * block_bytes + 16 * 1024 * 1024))


def _dot(a, b):
    return jnp.dot(a, b, preferred_element_type=F32)


def _sigmoid(x):
    return 1.0 / (1.0 + jnp.exp(-x))


def _layer_norm(y, g, b):
    mu = jnp.mean(y, axis=-1, keepdims=True)
    yc = y - mu
    var = jnp.mean(yc * yc, axis=-1, keepdims=True)
    return yc * lax.rsqrt(var + LN_EPS) * g + b


def _proj_kernel(x_ref, w_ref, lbl_ref, o_ref, wb_ref):
    j = pl.program_id(0)

    @pl.when(pl.program_id(1) == 0)
    def _():
        wb_ref[...] = w_ref[...].astype(BF16)

    def proj_block():
        return _dot(x_ref[...].astype(BF16), wb_ref[...])

    @pl.when(jnp.logical_and(j != PROJ_F_BLOCK, j != PROJ_GATE_BLOCK))
    def _():
        o_ref[...] = proj_block()

    @pl.when(j == PROJ_F_BLOCK)
    def _():
        lg = lbl_ref[...]
        ex = jnp.exp(lg - jnp.max(lg, axis=0, keepdims=True))
        lb = ex[0:1, :] / jnp.sum(ex, axis=0, keepdims=True)
        o_ref[...] = lb + (1.0 - lb) * _sigmoid(proj_block())

    @pl.when(j == PROJ_GATE_BLOCK)
    def _():
        g = proj_block()
        o_ref[...] = g * _sigmoid(g)


def _proj(x2d, w, lb_logits, tm=1024, tn=1024):
    t, d = x2d.shape
    n = w.shape[1]
    assert tn == HGRN_WIDTH and n == IN_COLS
    blocks = tm * d * 4 + d * tn * 4 + tm * tn * 4
    return pl.pallas_call(
        _proj_kernel,
        grid=(n // tn, t // tm),
        in_specs=[pl.BlockSpec((tm, d), lambda j, i: (i, 0)),
                  pl.BlockSpec((d, tn), lambda j, i: (0, j)),
                  pl.BlockSpec((2, tn), lambda j, i: (0, 0))],
        out_specs=pl.BlockSpec((tm, tn), lambda j, i: (i, j)),
        out_shape=jax.ShapeDtypeStruct((t, n), F32),
        scratch_shapes=[pltpu.VMEM((d, tn), BF16)],
        compiler_params=pltpu.CompilerParams(
            dimension_semantics=("arbitrary", "arbitrary"),
            vmem_limit_bytes=_vmem_limit(blocks + d * tn * 2)),
        name="proj",
    )(x2d, w, lb_logits)


def _pool_kernel(u_ref, w_ref, pb_ref, ps_ref, o_ref, buf_ref, lvl_ref, *, ts):
    i = pl.program_id(1)
    rows = ts + POOL_HALO

    @pl.when(i == 0)
    def _():
        buf_ref[0:POOL_HALO, :] = jnp.zeros((POOL_HALO, POOL_WIDTH), F32)

    @pl.when(i > 0)
    def _():
        buf_ref[0:POOL_HALO, :] = buf_ref[ts:ts + POOL_HALO, :]

    buf_ref[POOL_HALO:POOL_HALO + ts, :] = u_ref[0]

    pos = i * ts + lax.broadcasted_iota(jnp.int32, (ts, POOL_GROUP_DIM), 0)
    for gi, w in enumerate(POOL_WINDOWS):
        cols = slice(gi * POOL_GROUP_DIM, (gi + 1) * POOL_GROUP_DIM)
        cur = buf_ref[POOL_HALO:rows, cols]
        n_levels = w.bit_length() - 1
        first = [POOL_HALO]
        for lvl in range(n_levels - 1, 0, -1):
            reach = first[0] - (1 << lvl)
            first.insert(0, reach - reach % V7X_SUBLANES)
        src, src_cols = buf_ref, cols
        for lvl in range(n_levels):
            lo, shift = first[lvl], 1 << lvl
            win = src[lo:rows, src_cols] + src[lo - shift:rows - shift, src_cols]
            if lvl < n_levels - 1:
                dst = lvl_ref.at[lvl % 2]
                dst[lo:rows, :] = win
                src, src_cols = dst, slice(None)
        cnt = jnp.minimum(pos + 1, w).astype(F32)
        pooled = win / cnt - cur
        y = _dot(pooled.astype(BF16), w_ref[gi]) + pb_ref[:, cols]
        o_ref[:, cols] = (y * ps_ref[:, cols]).astype(BF16)


def _pool(proj3, pool_w_bf, pool_b, pool_scale, ts=512):
    b, s, _ = proj3.shape
    nt = s // ts
    blocks = ts * POOL_WIDTH * 4 + 4 * 256 * 256 * 2 + ts * POOL_WIDTH * 2
    return pl.pallas_call(
        functools.partial(_pool_kernel, ts=ts),
        grid=(b, nt),
        in_specs=[pl.BlockSpec((1, ts, POOL_WIDTH), lambda bi, i: (bi, i, 0)),
                  pl.BlockSpec((4, 256, 256), lambda bi, i: (0, 0, 0)),
                  pl.BlockSpec((1, POOL_WIDTH), lambda bi, i: (0, 0)),
                  pl.BlockSpec((1, POOL_WIDTH), lambda bi, i: (0, 0))],
        out_specs=pl.BlockSpec((ts, POOL_WIDTH), lambda bi, i: (bi * nt + i, 0)),
        out_shape=jax.ShapeDtypeStruct((b * s, POOL_WIDTH), BF16),
        scratch_shapes=[pltpu.VMEM((ts + POOL_HALO, POOL_WIDTH), F32),
                        pltpu.VMEM((2, ts + POOL_HALO, POOL_GROUP_DIM), F32)],
        compiler_params=pltpu.CompilerParams(
            dimension_semantics=("arbitrary", "arbitrary"),
            vmem_limit_bytes=_vmem_limit(blocks + (ts + POOL_HALO) * (POOL_WIDTH + 512) * 4)),
        name="pool",
    )(proj3, pool_w_bf, pool_b, pool_scale)


def _hgrn_kernel(q_ref, f_ref, v_ref, sgate_ref, gn_ref, wu_ref, wd_ref, wo_ref,
                 o_ref, wub_ref, wdb_ref, wob_ref, st_ref, *, ts):
    nb = HGRN_BLOCK
    half = V7X_SUBLANES
    dk = HGRN_DK
    nh = HGRN_HEADS
    width = HGRN_WIDTH

    @pl.when(pl.program_id(0) == 0)
    def _():
        st_ref[...] = jnp.zeros(st_ref.shape, F32)

    wub_ref[...] = wu_ref[...].astype(BF16)
    wdb_ref[...] = wd_ref[...].astype(BF16)
    wob_ref[...] = wo_ref[...].astype(BF16)

    gn = gn_ref[...]

    rowi = lax.broadcasted_iota(jnp.int32, (half, width), 0)
    sel = (lax.broadcasted_iota(jnp.int32, (nb, nb * dk), 0)
           == lax.broadcasted_iota(jnp.int32, (nb, nb * dk), 1) // dk).astype(BF16)
    lane_head = lax.broadcasted_iota(jnp.int32, (nb, nh * nb), 1) // nb

    def block_cumsum(x):
        lo, hi = x[0:half, :], x[half:nb, :]
        sh = 1
        while sh < half:
            r_lo = pltpu.roll(lo, sh, axis=0)
            r_hi = pltpu.roll(hi, sh, axis=0)
            lo, hi = (lo + jnp.where(rowi >= sh, r_lo, 0.0),
                      hi + jnp.where(rowi >= sh, r_hi, r_lo))
            sh *= 2
        return jnp.concatenate([lo, hi + lo], axis=0)

    def front(n, bi):
        r0 = n * nb
        q = q_ref[bi, pl.ds(r0, nb), :]
        f = f_ref[bi, pl.ds(r0, nb), :]
        kk = 1.0 - f
        k_lo, k_hi = kk[0:half, :], kk[half:nb, :]
        b = block_cumsum(jnp.log(f))
        a = (q * jnp.exp(b)).astype(BF16)
        dec = jnp.exp(b[nb - 1:nb, :])

        u_lo = None
        u_hi = None
        cols = []
        for t in range(nb):
            ft = jnp.broadcast_to(f[t:t + 1, :], (half, width))
            qt = jnp.broadcast_to(q[t:t + 1, :], (half, width))
            if t == 0:
                u_lo = jnp.where(rowi == 0, 1.0, 0.0)
            elif t < half:
                u_lo = jnp.where(rowi == t, 1.0, u_lo * ft)
            else:
                u_lo = u_lo * ft
                u_hi = jnp.where(rowi == t - half, 1.0, u_hi * ft if t > half else 0.0)
            w_lo = u_lo * k_lo
            w_hi = u_hi * k_hi if t >= half else jnp.zeros((half, width), F32)
            cols.append(jnp.concatenate([w_lo * qt, w_hi * qt], axis=0).astype(BF16))
        kx = jnp.concatenate([w_lo, w_hi], axis=0).astype(BF16)

        pcat = jnp.concatenate(
            [jnp.concatenate([c[:, h * dk:(h + 1) * dk] for c in cols], axis=1)
             for h in range(nh)], axis=0)
        sc = lax.dot_general(sel, pcat, (((1,), (1,)), ((), ())),
                             preferred_element_type=F32)
        return sc, a, kx, dec

    def back(n, bi, sc, a, kx, dec):
        r0 = n * nb
        v = v_ref[bi, pl.ds(r0, nb), :]
        sgate = sgate_ref[bi, pl.ds(r0, nb), :]
        sc_heads = jnp.concatenate(
            [jnp.where(lane_head == h, sc, 0.0) for h in range(nh)], axis=0).astype(BF16)
        v_rows = jnp.concatenate([v[:, h * dk:(h + 1) * dk] for h in range(nh)],
                                 axis=0).astype(BF16)
        o_intra = _dot(sc_heads, v_rows)

        outs = []
        for h in range(nh):
            hs = slice(h * dk, (h + 1) * dk)
            st = st_ref[bi, h]
            o_h = lax.dot_general(a[:, hs], st.astype(BF16), (((1,), (1,)), ((), ())),
                                  preferred_element_type=F32)
            o_h = o_h + o_intra[h * nb:(h + 1) * nb, :]
            ms = jnp.mean(o_h * o_h, axis=-1, keepdims=True)
            outs.append(o_h * lax.rsqrt(ms + RMS_EPS))
            upd = _dot(v[:, hs].T.astype(BF16), kx[:, hs])
            st_ref[bi, h] = st * dec[:, hs] + upd

        o = jnp.concatenate(outs, axis=1) * gn * sgate
        o_ref[bi, pl.ds(r0, nb), :] = o.astype(BF16)

    seqs = range(q_ref.shape[0])

    def step(n, carry):
        nxt = tuple(front(n + 1, bi) for bi in seqs)
        for bi in seqs:
            back(n, bi, *carry[bi])
        return nxt

    carry = tuple(front(0, bi) for bi in seqs)
    for n in range(ts // nb - 1):
        carry = step(n, carry)
    for bi in seqs:
        back(ts // nb - 1, bi, *carry[bi])


def _hgrn(proj3, g_norm, side_weights, ts=256):
    b, s, _ = proj3.shape
    nt = s // ts
    w = HGRN_WIDTH

    def col(c):
        return pl.BlockSpec((b, ts, w), lambda i: (0, i, c))

    def slab(wt):
        assert wt.shape[0] % (nt * 2 * V7X_SUBLANES) == 0
        return pl.BlockSpec((wt.shape[0] // nt, wt.shape[1]), lambda i: (i, 0))

    side_elems = sum(wt.size for wt in side_weights) // nt
    blocks = b * (4 * ts * w * 4 + ts * w * 2) + side_elems * (4 + 2)
    out, *side_bf = pl.pallas_call(
        functools.partial(_hgrn_kernel, ts=ts),
        grid=(nt,),
        in_specs=[col(1), col(PROJ_F_BLOCK), col(3), col(PROJ_GATE_BLOCK),
                  pl.BlockSpec((1, w), lambda i: (0, 0))] + [slab(wt) for wt in side_weights],
        out_specs=[pl.BlockSpec((b, ts, w), lambda i: (0, i, 0))]
        + [slab(wt) for wt in side_weights],
        out_shape=[jax.ShapeDtypeStruct((b, s, w), BF16)]
        + [jax.ShapeDtypeStruct(wt.shape, BF16) for wt in side_weights],
        scratch_shapes=[pltpu.VMEM((b, HGRN_HEADS, HGRN_DK, HGRN_DK), F32)],
        compiler_params=pltpu.CompilerParams(
            dimension_semantics=("arbitrary",),
            vmem_limit_bytes=_vmem_limit(blocks)),
        name="hgrn",
    )(proj3, proj3, proj3, proj3, g_norm, *side_weights)
    return out.reshape(b * s, w), side_bf


def _mix_kernel(a_ref, b_ref, x_ref, wt_ref, wb_ref, g_ref, bt_ref, o_ref, ob_ref):
    mix = _dot(a_ref[...], wt_ref[...]) + _dot(b_ref[...], wb_ref[...])
    y = ALPHA * x_ref[...] + mix
    x1 = _layer_norm(y, g_ref[...], bt_ref[...])
    o_ref[...] = x1
    ob_ref[...] = x1.astype(BF16)


def _mix(a_out, b_out, x2d, w_out_bf, ln_g, ln_b, tm=512):
    t, d = x2d.shape
    hw = a_out.shape[1]
    blocks = 2 * tm * hw * 2 + 2 * tm * d * 4 + tm * d * 2 + d * d * 2
    return pl.pallas_call(
        _mix_kernel,
        grid=(t // tm,),
        in_specs=[pl.BlockSpec((tm, hw), lambda i: (i, 0)),
                  pl.BlockSpec((tm, hw), lambda i: (i, 0)),
                  pl.BlockSpec((tm, d), lambda i: (i, 0)),
                  pl.BlockSpec((hw, d), lambda i: (0, 0)),
                  pl.BlockSpec((hw, d), lambda i: (1, 0)),
                  pl.BlockSpec((1, d), lambda i: (0, 0)),
                  pl.BlockSpec((1, d), lambda i: (0, 0))],
        out_specs=[pl.BlockSpec((tm, d), lambda i: (i, 0)),
                   pl.BlockSpec((tm, d), lambda i: (i, 0))],
        out_shape=[jax.ShapeDtypeStruct((t, d), F32), jax.ShapeDtypeStruct((t, d), BF16)],
        compiler_params=pltpu.CompilerParams(
            dimension_semantics=("arbitrary",),
            vmem_limit_bytes=_vmem_limit(blocks)),
        name="mix",
    )(a_out, b_out, x2d, w_out_bf, w_out_bf, ln_g, ln_b)


def _ffn_kernel(xb_ref, xres_ref, wg_ref, wv_ref, cwg_ref, cwv_ref, cbg_ref, cbv_ref, wd_ref,
                g_ref, bt_ref, o_ref, ug_ref, uv_ref, tail_ref, h_ref,
                *, tm, tiles_per_seq, nf, ns):
    s = pl.program_id(0)
    su = jnp.minimum(s, ns - 1)
    iu = su // nf
    ju = su % nf
    jd = jnp.maximum(s - 1, 0) % nf
    halo = CONV_HALO

    @pl.when(s == 0)
    def _():
        tail_ref[...] = jnp.zeros(tail_ref.shape, F32)
        h_ref[...] = jnp.zeros(h_ref.shape, BF16)

    @pl.when(jd == 0)
    def _():
        o_ref[...] = jnp.zeros(o_ref.shape, F32)

    seq_start = (iu % tiles_per_seq) == 0

    def conv(w_ref, cw_ref, cb_ref, ubuf_ref, slot):
        u = _dot(xb_ref[...], w_ref[...])
        ubuf_ref[0:halo, :] = jnp.where(seq_start, 0.0, tail_ref[ju, slot])
        ubuf_ref[halo:halo + tm, :] = u
        u1 = ubuf_ref[halo - 1:halo - 1 + tm, :]
        u2 = ubuf_ref[halo - 2:halo - 2 + tm, :]
        tail_ref[ju, slot] = ubuf_ref[tm:tm + halo, :]
        cw = cw_ref[...]
        return cb_ref[...] + cw[0:1, :] * u2 + cw[1:2, :] * u1 + cw[2:3, :] * u

    def body(h_prev_ref, h_next_ref):
        gc = conv(wg_ref, cwg_ref, cbg_ref, ug_ref, 0)
        vc = conv(wv_ref, cwv_ref, cbv_ref, uv_ref, 1)
        o_ref[...] += _dot(h_prev_ref[...], wd_ref[...])
        h_next_ref[...] = ((gc * _sigmoid(gc)) * vc).astype(BF16)

    @pl.when(s % 2 == 0)
    def _():
        body(h_ref.at[1], h_ref.at[0])

    @pl.when(s % 2 == 1)
    def _():
        body(h_ref.at[0], h_ref.at[1])

    for c in range(o_ref.shape[1] // FFN_RES_COLS):
        @pl.when(jd == c)
        def _():
            cols = slice(c * FFN_RES_COLS, (c + 1) * FFN_RES_COLS)
            o_ref[:, cols] += ALPHA * xres_ref[...]

    @pl.when(jnp.logical_and(jd == nf - 1, s > 0))
    def _():
        o_ref[...] = _layer_norm(o_ref[...], g_ref[...], bt_ref[...])


def _ffn(x1, x1_bf, w_up_bf, conv_w, conv_b, w_down_bf, ln_g, ln_b, seq, tm=1024, fc=512):
    t, d = x1.shape
    nf = D_FF // fc
    ns = (t // tm) * nf
    n_res = d // FFN_RES_COLS
    assert n_res <= nf

    def up(s):
        return jnp.minimum(s, ns - 1)

    def down(s):
        return jnp.maximum(s - 1, 0)

    blocks = (tm * d * 2 + tm * FFN_RES_COLS * 4 + tm * d * 4 + 2 * d * fc * 2 + fc * d * 2
              + 8 * fc * 4)
    scratch = (2 * (tm + CONV_HALO) * fc * 4 + nf * 2 * CONV_HALO * fc * 4 + 2 * tm * fc * 2)
    return pl.pallas_call(
        functools.partial(_ffn_kernel, tm=tm, tiles_per_seq=seq // tm, nf=nf, ns=ns),
        grid=(ns + 1,),
        in_specs=[pl.BlockSpec((tm, d), lambda s: (up(s) // nf, 0)),
                  pl.BlockSpec((tm, FFN_RES_COLS),
                               lambda s: (down(s) // nf, jnp.minimum(down(s) % nf, n_res - 1))),
                  pl.BlockSpec((d, fc), lambda s: (0, up(s) % nf)),
                  pl.BlockSpec((d, fc), lambda s: (0, up(s) % nf + nf)),
                  pl.BlockSpec((CONV_WIDTH, fc), lambda s: (0, up(s) % nf)),
                  pl.BlockSpec((CONV_WIDTH, fc), lambda s: (0, up(s) % nf + nf)),
                  pl.BlockSpec((1, fc), lambda s: (0, up(s) % nf)),
                  pl.BlockSpec((1, fc), lambda s: (0, up(s) % nf + nf)),
                  pl.BlockSpec((fc, d), lambda s: (down(s) % nf, 0)),
                  pl.BlockSpec((1, d), lambda s: (0, 0)),
                  pl.BlockSpec((1, d), lambda s: (0, 0))],
        out_specs=pl.BlockSpec((tm, d), lambda s: (down(s) // nf, 0)),
        out_shape=jax.ShapeDtypeStruct((t, d), F32),
        scratch_shapes=[pltpu.VMEM((tm + CONV_HALO, fc), F32),
                        pltpu.VMEM((tm + CONV_HALO, fc), F32),
                        pltpu.VMEM((nf, 2, CONV_HALO, fc), F32),
                        pltpu.VMEM((2, tm, fc), BF16)],
        compiler_params=pltpu.CompilerParams(
            dimension_semantics=("arbitrary",),
            vmem_limit_bytes=_vmem_limit(blocks + scratch)),
        name="ffn",
    )(x1_bf, x1, w_up_bf, w_up_bf, conv_w, conv_w, conv_b, conv_b, w_down_bf, ln_g, ln_b)


def kernel(x, w_in, pool_w, pool_b, pool_scale, hgrn_lb_logits, hgrn_g_norm, w_out,
           ln1_g, ln1_b, w_up, conv_w, conv_b, w_down, ln2_g, ln2_b):
    bsz, seq, d = x.shape
    assert (d, w_in.shape[0], hgrn_lb_logits.shape[0]) == (D_MODEL, DEPTH, DEPTH + 1)
    t = bsz * seq
    x2d = x.reshape(t, d)

    proj = _proj(x2d, w_in[0], hgrn_lb_logits)
    proj3 = proj.reshape(bsz, seq, IN_COLS)
    a_out = _pool(proj3, pool_w[0].astype(BF16), pool_b[0].reshape(1, POOL_WIDTH),
                  pool_scale[0].reshape(1, POOL_WIDTH))
    b_out, (w_up_bf, w_down_bf, w_out_bf) = _hgrn(
        proj3, hgrn_g_norm[0].reshape(1, HGRN_WIDTH),
        (w_up[0], w_down[0], w_out[0]))
    x1, x1_bf = _mix(a_out, b_out, x2d, w_out_bf, ln1_g[0].reshape(1, d), ln1_b[0].reshape(1, d))
    out = _ffn(x1, x1_bf, w_up_bf, conv_w[0], conv_b[0].reshape(1, 2 * D_FF),
               w_down_bf, ln2_g[0].reshape(1, d), ln2_b[0].reshape(1, d), seq)
    return out.reshape(bsz, seq, d)
```

```python
import functools

import jax
import jax.numpy as jnp
from jax import lax
from jax.experimental import pallas as pl
from jax.experimental.pallas import tpu as pltpu

D_MODEL = 2048
POOL_WIDTH = 1024
POOL_WINDOWS = (2, 4, 8, 16)
POOL_GROUP_DIM = 256
HGRN_WIDTH = 1024
HGRN_HEADS = 8
HGRN_DK = 128
IN_COLS = 5120
PROJ_F_BLOCK = 2
PROJ_GATE_BLOCK = 4
D_FF = 5632
CONV_WIDTH = 3
DEPTH = 1
ALPHA = (2.0 * DEPTH) ** 0.25
LN_EPS = 1e-5
RMS_EPS = 1e-6

V7X_LANES = 128
V7X_SUBLANES = 8
V7X_VMEM_BYTES = 64 * 1024 * 1024

HGRN_BLOCK = 16
POOL_HALO = 32
CONV_HALO = V7X_SUBLANES
FFN_RES_COLS = 256

BF16 = jnp.bfloat16
F32 = jnp.float32


def _vmem_limit(block_bytes):
    return int(min(V7X_VMEM_BYTES * 7 // 8, 2 * block_bytes + 16 * 1024 * 1024))


def _dot(a, b):
    return jnp.dot(a, b, preferred_element_type=F32)


def _sigmoid(x):
    return 1.0 / (1.0 + jnp.exp(-x))


def _layer_norm(y, g, b):
    mu = jnp.mean(y, axis=-1, keepdims=True)
    yc = y - mu
    var = jnp.mean(yc * yc, axis=-1, keepdims=True)
    return yc * lax.rsqrt(var + LN_EPS) * g + b


def _proj_kernel(x_ref, w_ref, lbl_ref, o_ref, wb_ref):
    j = pl.program_id(0)

    @pl.when(pl.program_id(1) == 0)
    def _():
        wb_ref[...] = w_ref[...].astype(BF16)

    def proj_block():
        return _dot(x_ref[...].astype(BF16), wb_ref[...])

    @pl.when(jnp.logical_and(j != PROJ_F_BLOCK, j != PROJ_GATE_BLOCK))
    def _():
        o_ref[...] = proj_block()

    @pl.when(j == PROJ_F_BLOCK)
    def _():
        lg = lbl_ref[...]
        ex = jnp.exp(lg - jnp.max(lg, axis=0, keepdims=True))
        lb = ex[0:1, :] / jnp.sum(ex, axis=0, keepdims=True)
        o_ref[...] = lb + (1.0 - lb) * _sigmoid(proj_block())

    @pl.when(j == PROJ_GATE_BLOCK)
    def _():
        g = proj_block()
        o_ref[...] = g * _sigmoid(g)


def _proj(x2d, w, lb_logits, tm=1024, tn=1024):
    t, d = x2d.shape
    n = w.shape[1]
    assert tn == HGRN_WIDTH and n == IN_COLS
    blocks = tm * d * 4 + d * tn * 4 + tm * tn * 4
    return pl.pallas_call(
        _proj_kernel,
        grid=(n // tn, t // tm),
        in_specs=[pl.BlockSpec((tm, d), lambda j, i: (i, 0)),
                  pl.BlockSpec((d, tn), lambda j, i: (0, j)),
                  pl.BlockSpec((2, tn), lambda j, i: (0, 0))],
        out_specs=pl.BlockSpec((tm, tn), lambda j, i: (i, j)),
        out_shape=jax.ShapeDtypeStruct((t, n), F32),
        scratch_shapes=[pltpu.VMEM((d, tn), BF16)],
        compiler_params=pltpu.CompilerParams(
            dimension_semantics=("arbitrary", "arbitrary"),
            vmem_limit_bytes=_vmem_limit(blocks + d * tn * 2)),
        name="proj",
    )(x2d, w, lb_logits)


def _pool_kernel(u_ref, w_ref, pb_ref, ps_ref, o_ref, buf_ref, lvl_ref, *, ts):
    i = pl.program_id(1)
    rows = ts + POOL_HALO

    @pl.when(i == 0)
    def _():
        buf_ref[0:POOL_HALO, :] = jnp.zeros((POOL_HALO, POOL_WIDTH), F32)

    @pl.when(i > 0)
    def _():
        buf_ref[0:POOL_HALO, :] = buf_ref[ts:ts + POOL_HALO, :]

    buf_ref[POOL_HALO:POOL_HALO + ts, :] = u_ref[0]

    pos = i * ts + lax.broadcasted_iota(jnp.int32, (ts, POOL_GROUP_DIM), 0)
    for gi, w in enumerate(POOL_WINDOWS):
        cols = slice(gi * POOL_GROUP_DIM, (gi + 1) * POOL_GROUP_DIM)
        cur = buf_ref[POOL_HALO:rows, cols]
        n_levels = w.bit_length() - 1
        first = [POOL_HALO]
        for lvl in range(n_levels - 1, 0, -1):
            reach = first[0] - (1 << lvl)
            first.insert(0, reach - reach % V7X_SUBLANES)
        src, src_cols = buf_ref, cols
        for lvl in range(n_levels):
            lo, shift = first[lvl], 1 << lvl
            win = src[lo:rows, src_cols] + src[lo - shift:rows - shift, src_cols]
            if lvl < n_levels - 1:
                dst = lvl_ref.at[lvl % 2]
                dst[lo:rows, :] = win
                src, src_cols = dst, slice(None)
        cnt = jnp.minimum(pos + 1, w).astype(F32)
        pooled = win / cnt - cur
        y = _dot(pooled.astype(BF16), w_ref[gi].astype(BF16)) + pb_ref[gi:gi + 1, :]
        o_ref[:, cols] = (y * ps_ref[:, cols]).astype(BF16)


def _pool(proj3, pool_w, pool_b, pool_scale, ts=512):
    b, s, _ = proj3.shape
    nt = s // ts
    blocks = ts * POOL_WIDTH * 4 + pool_w.size * 4 + ts * POOL_WIDTH * 2
    return pl.pallas_call(
        functools.partial(_pool_kernel, ts=ts),
        grid=(b, nt),
        in_specs=[pl.BlockSpec((1, ts, POOL_WIDTH), lambda bi, i: (bi, i, 0)),
                  pl.BlockSpec(pool_w.shape, lambda bi, i: (0, 0, 0)),
                  pl.BlockSpec(pool_b.shape, lambda bi, i: (0, 0)),
                  pl.BlockSpec((1, POOL_WIDTH), lambda bi, i: (0, 0))],
        out_specs=pl.BlockSpec((ts, POOL_WIDTH), lambda bi, i: (bi * nt + i, 0)),
        out_shape=jax.ShapeDtypeStruct((b * s, POOL_WIDTH), BF16),
        scratch_shapes=[pltpu.VMEM((ts + POOL_HALO, POOL_WIDTH), F32),
                        pltpu.VMEM((2, ts + POOL_HALO, POOL_GROUP_DIM), F32)],
        compiler_params=pltpu.CompilerParams(
            dimension_semantics=("arbitrary", "arbitrary"),
            vmem_limit_bytes=_vmem_limit(blocks + (ts + POOL_HALO) * (POOL_WIDTH + 512) * 4)),
        name="pool",
    )(proj3, pool_w, pool_b, pool_scale)


def _hgrn_kernel(q_ref, f_ref, v_ref, sgate_ref, gn_ref, wu_ref, wd_ref, wo_ref,
                 o_ref, wub_ref, wdb_ref, wob_ref, st_ref, *, ts):
    nb = HGRN_BLOCK
    half = V7X_SUBLANES
    dk = HGRN_DK
    nh = HGRN_HEADS
    width = HGRN_WIDTH

    @pl.when(pl.program_id(0) == 0)
    def _():
        st_ref[...] = jnp.zeros(st_ref.shape, F32)

    wub_ref[...] = wu_ref[...].astype(BF16)
    wdb_ref[...] = wd_ref[...].astype(BF16)
    wob_ref[...] = wo_ref[...].astype(BF16)

    gn = gn_ref[...]

    rowi = lax.broadcasted_iota(jnp.int32, (half, width), 0)
    sel = (lax.broadcasted_iota(jnp.int32, (nb, nb * dk), 0)
           == lax.broadcasted_iota(jnp.int32, (nb, nb * dk), 1) // dk).astype(BF16)
    lane_head = lax.broadcasted_iota(jnp.int32, (nb, nh * nb), 1) // nb

    def block_cumsum(x):
        lo, hi = x[0:half, :], x[half:nb, :]
        sh = 1
        while sh < half:
            r_lo = pltpu.roll(lo, sh, axis=0)
            r_hi = pltpu.roll(hi, sh, axis=0)
            lo, hi = (lo + jnp.where(rowi >= sh, r_lo, 0.0),
                      hi + jnp.where(rowi >= sh, r_hi, r_lo))
            sh *= 2
        return jnp.concatenate([lo, hi + lo], axis=0)

    def front(n, bi):
        r0 = n * nb
        q = q_ref[bi, pl.ds(r0, nb), :]
        f = f_ref[bi, pl.ds(r0, nb), :]
        kk = 1.0 - f
        k_lo, k_hi = kk[0:half, :], kk[half:nb, :]
        b = block_cumsum(jnp.log(f))
        a = (q * jnp.exp(b)).astype(BF16)
        dec = jnp.exp(b[nb - 1:nb, :])

        u_lo = None
        u_hi = None
        cols = []
        for t in range(nb):
            ft = jnp.broadcast_to(f[t:t + 1, :], (half, width))
            qt = jnp.broadcast_to(q[t:t + 1, :], (half, width))
            if t == 0:
                u_lo = jnp.where(rowi == 0, 1.0, 0.0)
            elif t < half:
                u_lo = jnp.where(rowi == t, 1.0, u_lo * ft)
            else:
                u_lo = u_lo * ft
                u_hi = jnp.where(rowi == t - half, 1.0, u_hi * ft if t > half else 0.0)
            w_lo = u_lo * k_lo
            w_hi = u_hi * k_hi if t >= half else jnp.zeros((half, width), F32)
            cols.append(jnp.concatenate([w_lo * qt, w_hi * qt], axis=0).astype(BF16))
        kx = jnp.concatenate([w_lo, w_hi], axis=0).astype(BF16)

        pcat = jnp.concatenate(
            [jnp.concatenate([c[:, h * dk:(h + 1) * dk] for c in cols], axis=1)
             for h in range(nh)], axis=0)
        sc = lax.dot_general(sel, pcat, (((1,), (1,)), ((), ())),
                             preferred_element_type=F32)
        return sc, a, kx, dec

    def back(n, bi, sc, a, kx, dec):
        r0 = n * nb
        v = v_ref[bi, pl.ds(r0, nb), :]
        sgate = sgate_ref[bi, pl.ds(r0, nb), :]
        sc_heads = jnp.concatenate(
            [jnp.where(lane_head == h, sc, 0.0) for h in range(nh)], axis=0).astype(BF16)
        v_rows = jnp.concatenate([v[:, h * dk:(h + 1) * dk] for h in range(nh)],
                                 axis=0).astype(BF16)
        o_intra = _dot(sc_heads, v_rows)

        outs = []
        for h in range(nh):
            hs = slice(h * dk, (h + 1) * dk)
            st = st_ref[bi, h]
            o_h = lax.dot_general(a[:, hs], st.astype(BF16), (((1,), (1,)), ((), ())),
                                  preferred_element_type=F32)
            o_h = o_h + o_intra[h * nb:(h + 1) * nb, :]
            ms = jnp.mean(o_h * o_h, axis=-1, keepdims=True)
            outs.append(o_h * lax.rsqrt(ms + RMS_EPS))
            upd = _dot(v[:, hs].T.astype(BF16), kx[:, hs])
            st_ref[bi, h] = st * dec[:, hs] + upd

        o = jnp.concatenate(outs, axis=1) * gn * sgate
        o_ref[bi, pl.ds(r0, nb), :] = o.astype(BF16)

    seqs = range(q_ref.shape[0])

    def step(n, carry):
        nxt = tuple(front(n + 1, bi) for bi in seqs)
        for bi in seqs:
            back(n, bi, *carry[bi])
        return nxt

    carry = tuple(front(0, bi) for bi in seqs)
    for n in range(ts // nb - 1):
        carry = step(n, carry)
    for bi in seqs:
        back(ts // nb - 1, bi, *carry[bi])


def _hgrn(proj3, g_norm, side_weights, ts=256):
    b, s, _ = proj3.shape
    nt = s // ts
    w = HGRN_WIDTH

    def col(c):
        return pl.BlockSpec((b, ts, w), lambda i: (0, i, c))

    def slab(wt):
        assert wt.shape[0] % (nt * 2 * V7X_SUBLANES) == 0
        return pl.BlockSpec((wt.shape[0] // nt, wt.shape[1]), lambda i: (i, 0))

    side_elems = sum(wt.size for wt in side_weights) // nt
    blocks = b * (4 * ts * w * 4 + ts * w * 2) + side_elems * (4 + 2)
    out, *side_bf = pl.pallas_call(
        functools.partial(_hgrn_kernel, ts=ts),
        grid=(nt,),
        in_specs=[col(1), col(PROJ_F_BLOCK), col(3), col(PROJ_GATE_BLOCK),
                  pl.BlockSpec((1, w), lambda i: (0, 0))] + [slab(wt) for wt in side_weights],
        out_specs=[pl.BlockSpec((b, ts, w), lambda i: (0, i, 0))]
        + [slab(wt) for wt in side_weights],
        out_shape=[jax.ShapeDtypeStruct((b, s, w), BF16)]
        + [jax.ShapeDtypeStruct(wt.shape, BF16) for wt in side_weights],
        scratch_shapes=[pltpu.VMEM((b, HGRN_HEADS, HGRN_DK, HGRN_DK), F32)],
        compiler_params=pltpu.CompilerParams(
            dimension_semantics=("arbitrary",),
            vmem_limit_bytes=_vmem_limit(blocks)),
        name="hgrn",
    )(proj3, proj3, proj3, proj3, g_norm, *side_weights)
    return out.reshape(b * s, w), side_bf


def _mix_kernel(a_ref, b_ref, x_ref, wt_ref, wb_ref, g_ref, bt_ref, o_ref, ob_ref):
    mix = _dot(a_ref[...], wt_ref[...]) + _dot(b_ref[...], wb_ref[...])
    y = ALPHA * x_ref[...] + mix
    x1 = _layer_norm(y, g_ref[...], bt_ref[...])
    o_ref[...] = x1
    ob_ref[...] = x1.astype(BF16)


def _mix(a_out, b_out, x2d, w_out_bf, ln_g, ln_b, tm=512):
    t, d = x2d.shape
    hw = a_out.shape[1]
    blocks = 2 * tm * hw * 2 + 2 * tm * d * 4 + tm * d * 2 + d * d * 2
    return pl.pallas_call(
        _mix_kernel,
        grid=(t // tm,),
        in_specs=[pl.BlockSpec((tm, hw), lambda i: (i, 0)),
                  pl.BlockSpec((tm, hw), lambda i: (i, 0)),
                  pl.BlockSpec((tm, d), lambda i: (i, 0)),
                  pl.BlockSpec((hw, d), lambda i: (0, 0)),
                  pl.BlockSpec((hw, d), lambda i: (1, 0)),
                  pl.BlockSpec((1, d), lambda i: (0, 0)),
                  pl.BlockSpec((1, d), lambda i: (0, 0))],
        out_specs=[pl.BlockSpec((tm, d), lambda i: (i, 0)),
                   pl.BlockSpec((tm, d), lambda i: (i, 0))],
        out_shape=[jax.ShapeDtypeStruct((t, d), F32), jax.ShapeDtypeStruct((t, d), BF16)],
        compiler_params=pltpu.CompilerParams(
            dimension_semantics=("arbitrary",),
            vmem_limit_bytes=_vmem_limit(blocks)),
        name="mix",
    )(a_out, b_out, x2d, w_out_bf, w_out_bf, ln_g, ln_b)


def _ffn_kernel(xb_ref, xres_ref, wg_ref, wv_ref, cwg_ref, cwv_ref, cbg_ref, cbv_ref, wd_ref,
                g_ref, bt_ref, o_ref, ug_ref, uv_ref, tail_ref, h_ref,
                *, tm, tiles_per_seq, nf, ns):
    s = pl.program_id(0)
    su = jnp.minimum(s, ns - 1)
    iu = su // nf
    ju = su % nf
    jd = jnp.maximum(s - 1, 0) % nf
    halo = CONV_HALO

    @pl.when(s == 0)
    def _():
        tail_ref[...] = jnp.zeros(tail_ref.shape, F32)
        h_ref[...] = jnp.zeros(h_ref.shape, BF16)

    @pl.when(jd == 0)
    def _():
        o_ref[...] = jnp.zeros(o_ref.shape, F32)

    seq_start = (iu % tiles_per_seq) == 0

    def conv(w_ref, cw_ref, cb_ref, ubuf_ref, slot):
        u = _dot(xb_ref[...], w_ref[...])
        ubuf_ref[0:halo, :] = jnp.where(seq_start, 0.0, tail_ref[ju, slot])
        ubuf_ref[halo:halo + tm, :] = u
        u1 = ubuf_ref[halo - 1:halo - 1 + tm, :]
        u2 = ubuf_ref[halo - 2:halo - 2 + tm, :]
        tail_ref[ju, slot] = ubuf_ref[tm:tm + halo, :]
        cw = cw_ref[...]
        return cb_ref[...] + cw[0:1, :] * u2 + cw[1:2, :] * u1 + cw[2:3, :] * u

    def body(h_prev_ref, h_next_ref):
        gc = conv(wg_ref, cwg_ref, cbg_ref, ug_ref, 0)
        vc = conv(wv_ref, cwv_ref, cbv_ref, uv_ref, 1)
        o_ref[...] += _dot(h_prev_ref[...], wd_ref[...])
        h_next_ref[...] = ((gc * _sigmoid(gc)) * vc).astype(BF16)

    @pl.when(s % 2 == 0)
    def _():
        body(h_ref.at[1], h_ref.at[0])

    @pl.when(s % 2 == 1)
    def _():
        body(h_ref.at[0], h_ref.at[1])

    for c in range(o_ref.shape[1] // FFN_RES_COLS):
        @pl.when(jd == c)
        def _():
            cols = slice(c * FFN_RES_COLS, (c + 1) * FFN_RES_COLS)
            o_ref[:, cols] += ALPHA * xres_ref[...]

    @pl.when(jnp.logical_and(jd == nf - 1, s > 0))
    def _():
        o_ref[...] = _layer_norm(o_ref[...], g_ref[...], bt_ref[...])


def _ffn(x1, x1_bf, w_up_bf, conv_w, conv_b, w_down_bf, ln_g, ln_b, seq, tm=1024, fc=512):
    t, d = x1.shape
    nf = D_FF // fc
    ns = (t // tm) * nf
    n_res = d // FFN_RES_COLS
    assert n_res <= nf

    def up(s):
        return jnp.minimum(s, ns - 1)

    def down(s):
        return jnp.maximum(s - 1, 0)

    blocks = (tm * d * 2 + tm * FFN_RES_COLS * 4 + tm * d * 4 + 2 * d * fc * 2 + fc * d * 2
              + 8 * fc * 4)
    scratch = (2 * (tm + CONV_HALO) * fc * 4 + nf * 2 * CONV_HALO * fc * 4 + 2 * tm * fc * 2)
    return pl.pallas_call(
        functools.partial(_ffn_kernel, tm=tm, tiles_per_seq=seq // tm, nf=nf, ns=ns),
        grid=(ns + 1,),
        in_specs=[pl.BlockSpec((tm, d), lambda s: (up(s) // nf, 0)),
                  pl.BlockSpec((tm, FFN_RES_COLS),
                               lambda s: (down(s) // nf, jnp.minimum(down(s) % nf, n_res - 1))),
                  pl.BlockSpec((d, fc), lambda s: (0, up(s) % nf)),
                  pl.BlockSpec((d, fc), lambda s: (0, up(s) % nf + nf)),
                  pl.BlockSpec((CONV_WIDTH, fc), lambda s: (0, up(s) % nf)),
                  pl.BlockSpec((CONV_WIDTH, fc), lambda s: (0, up(s) % nf + nf)),
                  pl.BlockSpec((1, fc), lambda s: (0, up(s) % nf)),
                  pl.BlockSpec((1, fc), lambda s: (0, up(s) % nf + nf)),
                  pl.BlockSpec((fc, d), lambda s: (down(s) % nf, 0)),
                  pl.BlockSpec((1, d), lambda s: (0, 0)),
                  pl.BlockSpec((1, d), lambda s: (0, 0))],
        out_specs=pl.BlockSpec((tm, d), lambda s: (down(s) // nf, 0)),
        out_shape=jax.ShapeDtypeStruct((t, d), F32),
        scratch_shapes=[pltpu.VMEM((tm + CONV_HALO, fc), F32),
                        pltpu.VMEM((tm + CONV_HALO, fc), F32),
                        pltpu.VMEM((nf, 2, CONV_HALO, fc), F32),
                        pltpu.VMEM((2, tm, fc), BF16)],
        compiler_params=pltpu.CompilerParams(
            dimension_semantics=("arbitrary",),
            vmem_limit_bytes=_vmem_limit(blocks + scratch)),
        name="ffn",
    )(x1_bf, x1, w_up_bf, w_up_bf, conv_w, conv_w, conv_b, conv_b, w_down_bf, ln_g, ln_b)


def kernel(x, w_in, pool_w, pool_b, pool_scale, hgrn_lb_logits, hgrn_g_norm, w_out,
           ln1_g, ln1_b, w_up, conv_w, conv_b, w_down, ln2_g, ln2_b):
    bsz, seq, d = x.shape
    assert (d, w_in.shape[0], hgrn_lb_logits.shape[0]) == (D_MODEL, DEPTH, DEPTH + 1)
    t = bsz * seq
    x2d = x.reshape(t, d)

    proj = _proj(x2d, w_in[0], hgrn_lb_logits)
    proj3 = proj.reshape(bsz, seq, IN_COLS)
    a_out = _pool(proj3, pool_w[0], pool_b[0], pool_scale[0].reshape(1, POOL_WIDTH))
    b_out, (w_up_bf, w_down_bf, w_out_bf) = _hgrn(
        proj3, hgrn_g_norm[0].reshape(1, HGRN_WIDTH),
        (w_up[0], w_down[0], w_out[0]))
    x1, x1_bf = _mix(a_out, b_out, x2d, w_out_bf, ln1_g[0].reshape(1, d), ln1_b[0].reshape(1, d))
    out = _ffn(x1, x1_bf, w_up_bf, conv_w[0], conv_b[0].reshape(1, 2 * D_FF),
               w_down_bf, ln2_g[0].reshape(1, d), ln2_b[0].reshape(1, d), seq)
    return out.reshape(bsz, seq, d)
```

```python
import functools

import jax
import jax.numpy as jnp
from jax import lax
from jax.experimental import pallas as pl
from jax.experimental.pallas import tpu as pltpu

D_MODEL = 2048
POOL_WIDTH = 1024
POOL_WINDOWS = (2, 4, 8, 16)
POOL_GROUP_DIM = 256
HGRN_WIDTH = 1024
HGRN_HEADS = 8
HGRN_DK = 128
IN_COLS = 5120
PROJ_F_BLOCK = 2
PROJ_GATE_BLOCK = 4
D_FF = 5632
CONV_WIDTH = 3
DEPTH = 1
ALPHA = (2.0 * DEPTH) ** 0.25
LN_EPS = 1e-5
RMS_EPS = 1e-6

V7X_LANES = 128
V7X_SUBLANES = 8
V7X_VMEM_BYTES = 64 * 1024 * 1024

HGRN_BLOCK = 16
POOL_HALO = 32
CONV_HALO = V7X_SUBLANES
FFN_RES_COLS = 256

BF16 = jnp.bfloat16
F32 = jnp.float32


def _vmem_limit(block_bytes):
    return int(min(V7X_VMEM_BYTES * 7 // 8, 2 * block_bytes + 16 * 1024 * 1024))


def _dot(a, b):
    return jnp.dot(a, b, preferred_element_type=F32)


def _sigmoid(x):
    return 1.0 / (1.0 + jnp.exp(-x))


def _layer_norm(y, g, b):
    mu = jnp.mean(y, axis=-1, keepdims=True)
    yc = y - mu
    var = jnp.mean(yc * yc, axis=-1, keepdims=True)
    return yc * lax.rsqrt(var + LN_EPS) * g + b


def _proj_head_kernel(x_ref, w_ref, o_ref, xb_ref, wb_ref):
    @pl.when(pl.program_id(0) == 0)
    def _():
        wb_ref[...] = w_ref[...].astype(BF16)

    xb = x_ref[...].astype(BF16)
    xb_ref[...] = xb
    o_ref[...] = _dot(xb, wb_ref[...])


def _proj_kernel(x_ref, w_ref, lbl_ref, o_ref, wb_ref):
    j = pl.program_id(0) + 1

    @pl.when(pl.program_id(1) == 0)
    def _():
        wb_ref[...] = w_ref[...].astype(BF16)

    def proj_block():
        return _dot(x_ref[...], wb_ref[...])

    @pl.when(jnp.logical_and(j != PROJ_F_BLOCK, j != PROJ_GATE_BLOCK))
    def _():
        o_ref[...] = proj_block()

    @pl.when(j == PROJ_F_BLOCK)
    def _():
        lg = lbl_ref[...]
        ex = jnp.exp(lg - jnp.max(lg, axis=0, keepdims=True))
        lb = ex[0:1, :] / jnp.sum(ex, axis=0, keepdims=True)
        o_ref[...] = lb + (1.0 - lb) * _sigmoid(proj_block())

    @pl.when(j == PROJ_GATE_BLOCK)
    def _():
        g = proj_block()
        o_ref[...] = g * _sigmoid(g)


def _proj(x2d, w, lb_logits, tm=1024, tn=1024):
    t, d = x2d.shape
    n = w.shape[1]
    assert tn == HGRN_WIDTH and n == IN_COLS
    head_blocks = tm * d * 4 + d * tn * 4 + tm * tn * 4 + tm * d * 2
    u_pool, x_bf = pl.pallas_call(
        _proj_head_kernel,
        grid=(t // tm,),
        in_specs=[pl.BlockSpec((tm, d), lambda i: (i, 0)),
                  pl.BlockSpec((d, tn), lambda i: (0, 0))],
        out_specs=[pl.BlockSpec((tm, tn), lambda i: (i, 0)),
                   pl.BlockSpec((tm, d), lambda i: (i, 0))],
        out_shape=[jax.ShapeDtypeStruct((t, tn), F32), jax.ShapeDtypeStruct((t, d), BF16)],
        scratch_shapes=[pltpu.VMEM((d, tn), BF16)],
        compiler_params=pltpu.CompilerParams(
            dimension_semantics=("arbitrary",),
            vmem_limit_bytes=_vmem_limit(head_blocks + d * tn * 2)),
        name="proj_head",
    )(x2d, w)
    blocks = tm * d * 2 + d * tn * 4 + tm * tn * 4
    rest = pl.pallas_call(
        _proj_kernel,
        grid=(n // tn - 1, t // tm),
        in_specs=[pl.BlockSpec((tm, d), lambda j, i: (i, 0)),
                  pl.BlockSpec((d, tn), lambda j, i: (0, j + 1)),
                  pl.BlockSpec((2, tn), lambda j, i: (0, 0))],
        out_specs=pl.BlockSpec((tm, tn), lambda j, i: (i, j)),
        out_shape=jax.ShapeDtypeStruct((t, n - tn), F32),
        scratch_shapes=[pltpu.VMEM((d, tn), BF16)],
        compiler_params=pltpu.CompilerParams(
            dimension_semantics=("arbitrary", "arbitrary"),
            vmem_limit_bytes=_vmem_limit(blocks + d * tn * 2)),
        name="proj",
    )(x_bf, w, lb_logits)
    return u_pool, rest


def _pool_kernel(u_ref, w_ref, pb_ref, ps_ref, o_ref, buf_ref, lvl_ref, *, ts):
    i = pl.program_id(1)
    rows = ts + POOL_HALO

    @pl.when(i == 0)
    def _():
        buf_ref[0:POOL_HALO, :] = jnp.zeros((POOL_HALO, POOL_WIDTH), F32)

    @pl.when(i > 0)
    def _():
        buf_ref[0:POOL_HALO, :] = buf_ref[ts:ts + POOL_HALO, :]

    buf_ref[POOL_HALO:POOL_HALO + ts, :] = u_ref[0]

    pos = i * ts + lax.broadcasted_iota(jnp.int32, (ts, POOL_GROUP_DIM), 0)
    for gi, w in enumerate(POOL_WINDOWS):
        cols = slice(gi * POOL_GROUP_DIM, (gi + 1) * POOL_GROUP_DIM)
        cur = buf_ref[POOL_HALO:rows, cols]
        n_levels = w.bit_length() - 1
        first = [POOL_HALO]
        for lvl in range(n_levels - 1, 0, -1):
            reach = first[0] - (1 << lvl)
            first.insert(0, reach - reach % V7X_SUBLANES)
        src, src_cols = buf_ref, cols
        for lvl in range(n_levels):
            lo, shift = first[lvl], 1 << lvl
            win = src[lo:rows, src_cols] + src[lo - shift:rows - shift, src_cols]
            if lvl < n_levels - 1:
                dst = lvl_ref.at[lvl % 2]
                dst[lo:rows, :] = win
                src, src_cols = dst, slice(None)
        cnt = jnp.minimum(pos + 1, w).astype(F32)
        pooled = win / cnt - cur
        y = _dot(pooled.astype(BF16), w_ref[gi].astype(BF16)) + pb_ref[gi:gi + 1, :]
        o_ref[:, cols] = (y * ps_ref[:, cols]).astype(BF16)


def _pool(proj3, pool_w, pool_b, pool_scale, ts=512):
    b, s, _ = proj3.shape
    nt = s // ts
    blocks = ts * POOL_WIDTH * 4 + pool_w.size * 4 + ts * POOL_WIDTH * 2
    return pl.pallas_call(
        functools.partial(_pool_kernel, ts=ts),
        grid=(b, nt),
        in_specs=[pl.BlockSpec((1, ts, POOL_WIDTH), lambda bi, i: (bi, i, 0)),
                  pl.BlockSpec(pool_w.shape, lambda bi, i: (0, 0, 0)),
                  pl.BlockSpec(pool_b.shape, lambda bi, i: (0, 0)),
                  pl.BlockSpec((1, POOL_WIDTH), lambda bi, i: (0, 0))],
        out_specs=pl.BlockSpec((ts, POOL_WIDTH), lambda bi, i: (bi * nt + i, 0)),
        out_shape=jax.ShapeDtypeStruct((b * s, POOL_WIDTH), BF16),
        scratch_shapes=[pltpu.VMEM((ts + POOL_HALO, POOL_WIDTH), F32),
                        pltpu.VMEM((2, ts + POOL_HALO, POOL_GROUP_DIM), F32)],
        compiler_params=pltpu.CompilerParams(
            dimension_semantics=("arbitrary", "arbitrary"),
            vmem_limit_bytes=_vmem_limit(blocks + (ts + POOL_HALO) * (POOL_WIDTH + 512) * 4)),
        name="pool",
    )(proj3, pool_w, pool_b, pool_scale)


def _hgrn_kernel(q_ref, f_ref, v_ref, sgate_ref, gn_ref, wu_ref, wd_ref, wo_ref,
                 o_ref, wub_ref, wdb_ref, wob_ref, st_ref, *, ts):
    nb = HGRN_BLOCK
    half = V7X_SUBLANES
    dk = HGRN_DK
    nh = HGRN_HEADS
    width = HGRN_WIDTH

    @pl.when(pl.program_id(0) == 0)
    def _():
        st_ref[...] = jnp.zeros(st_ref.shape, F32)

    wub_ref[...] = wu_ref[...].astype(BF16)
    wdb_ref[...] = wd_ref[...].astype(BF16)
    wob_ref[...] = wo_ref[...].astype(BF16)

    gn = gn_ref[...]

    rowi = lax.broadcasted_iota(jnp.int32, (half, width), 0)
    sel = (lax.broadcasted_iota(jnp.int32, (nb, nb * dk), 0)
           == lax.broadcasted_iota(jnp.int32, (nb, nb * dk), 1) // dk).astype(BF16)
    lane_head = lax.broadcasted_iota(jnp.int32, (nb, nh * nb), 1) // nb

    def block_cumsum(x):
        lo, hi = x[0:half, :], x[half:nb, :]
        sh = 1
        while sh < half:
            r_lo = pltpu.roll(lo, sh, axis=0)
            r_hi = pltpu.roll(hi, sh, axis=0)
            lo, hi = (lo + jnp.where(rowi >= sh, r_lo, 0.0),
                      hi + jnp.where(rowi >= sh, r_hi, r_lo))
            sh *= 2
        return jnp.concatenate([lo, hi + lo], axis=0)

    def front(n, bi):
        r0 = n * nb
        q = q_ref[bi, pl.ds(r0, nb), :]
        f = f_ref[bi, pl.ds(r0, nb), :]
        kk = 1.0 - f
        k_lo, k_hi = kk[0:half, :], kk[half:nb, :]
        b = block_cumsum(jnp.log(f))
        a = (q * jnp.exp(b)).astype(BF16)
        dec = jnp.exp(b[nb - 1:nb, :])

        u_lo = None
        u_hi = None
        cols = []
        for t in range(nb):
            ft = jnp.broadcast_to(f[t:t + 1, :], (half, width))
            qt = jnp.broadcast_to(q[t:t + 1, :], (half, width))
            if t == 0:
                u_lo = jnp.where(rowi == 0, 1.0, 0.0)
            elif t < half:
                u_lo = jnp.where(rowi == t, 1.0, u_lo * ft)
            else:
                u_lo = u_lo * ft
                u_hi = jnp.where(rowi == t - half, 1.0, u_hi * ft if t > half else 0.0)
            w_lo = u_lo * k_lo
            w_hi = u_hi * k_hi if t >= half else jnp.zeros((half, width), F32)
            cols.append(jnp.concatenate([w_lo * qt, w_hi * qt], axis=0).astype(BF16))
        kx = jnp.concatenate([w_lo, w_hi], axis=0).astype(BF16)

        pcat = jnp.concatenate(
            [jnp.concatenate([c[:, h * dk:(h + 1) * dk] for c in cols], axis=1)
             for h in range(nh)], axis=0)
        sc = lax.dot_general(sel, pcat, (((1,), (1,)), ((), ())),
                             preferred_element_type=F32)
        return sc, a, kx, dec

    def back(n, bi, sc, a, kx, dec):
        r0 = n * nb
        v = v_ref[bi, pl.ds(r0, nb), :]
        sgate = sgate_ref[bi, pl.ds(r0, nb), :]
        sc_heads = jnp.concatenate(
            [jnp.where(lane_head == h, sc, 0.0) for h in range(nh)], axis=0).astype(BF16)
        v_rows = jnp.concatenate([v[:, h * dk:(h + 1) * dk] for h in range(nh)],
                                 axis=0).astype(BF16)
        o_intra = _dot(sc_heads, v_rows)

        outs = []
        for h in range(nh):
            hs = slice(h * dk, (h + 1) * dk)
            st = st_ref[bi, h]
            o_h = lax.dot_general(a[:, hs], st.astype(BF16), (((1,), (1,)), ((), ())),
                                  preferred_element_type=F32)
            o_h = o_h + o_intra[h * nb:(h + 1) * nb, :]
            ms = jnp.mean(o_h * o_h, axis=-1, keepdims=True)
            outs.append(o_h * lax.rsqrt(ms + RMS_EPS))
            upd = _dot(v[:, hs].T.astype(BF16), kx[:, hs])
            st_ref[bi, h] = st * dec[:, hs] + upd

        o = jnp.concatenate(outs, axis=1) * gn * sgate
        o_ref[bi, pl.ds(r0, nb), :] = o.astype(BF16)

    seqs = range(q_ref.shape[0])

    def step(n, carry):
        nxt = tuple(front(n + 1, bi) for bi in seqs)
        for bi in seqs:
            back(n, bi, *carry[bi])
        return nxt

    carry = tuple(front(0, bi) for bi in seqs)
    for n in range(ts // nb - 1):
        carry = step(n, carry)
    for bi in seqs:
        back(ts // nb - 1, bi, *carry[bi])


def _hgrn(proj3, g_norm, side_weights, ts=256):
    b, s, _ = proj3.shape
    nt = s // ts
    w = HGRN_WIDTH

    def col(c):
        return pl.BlockSpec((b, ts, w), lambda i: (0, i, c))

    def slab(wt):
        assert wt.shape[0] % (nt * 2 * V7X_SUBLANES) == 0
        return pl.BlockSpec((wt.shape[0] // nt, wt.shape[1]), lambda i: (i, 0))

    side_elems = sum(wt.size for wt in side_weights) // nt
    blocks = b * (4 * ts * w * 4 + ts * w * 2) + side_elems * (4 + 2)
    out, *side_bf = pl.pallas_call(
        functools.partial(_hgrn_kernel, ts=ts),
        grid=(nt,),
        in_specs=[col(0), col(PROJ_F_BLOCK - 1), col(2), col(PROJ_GATE_BLOCK - 1),
                  pl.BlockSpec((1, w), lambda i: (0, 0))] + [slab(wt) for wt in side_weights],
        out_specs=[pl.BlockSpec((b, ts, w), lambda i: (0, i, 0))]
        + [slab(wt) for wt in side_weights],
        out_shape=[jax.ShapeDtypeStruct((b, s, w), BF16)]
        + [jax.ShapeDtypeStruct(wt.shape, BF16) for wt in side_weights],
        scratch_shapes=[pltpu.VMEM((b, HGRN_HEADS, HGRN_DK, HGRN_DK), F32)],
        compiler_params=pltpu.CompilerParams(
            dimension_semantics=("arbitrary",),
            vmem_limit_bytes=_vmem_limit(blocks)),
        name="hgrn",
    )(proj3, proj3, proj3, proj3, g_norm, *side_weights)
    return out.reshape(b * s, w), side_bf


def _mix_kernel(a_ref, b_ref, x_ref, wt_ref, wb_ref, g_ref, bt_ref, o_ref, ob_ref):
    mix = _dot(a_ref[...], wt_ref[...]) + _dot(b_ref[...], wb_ref[...])
    y = ALPHA * x_ref[...] + mix
    x1 = _layer_norm(y, g_ref[...], bt_ref[...])
    o_ref[...] = x1
    ob_ref[...] = x1.astype(BF16)


def _mix(a_out, b_out, x2d, w_out_bf, ln_g, ln_b, tm=512):
    t, d = x2d.shape
    hw = a_out.shape[1]
    blocks = 2 * tm * hw * 2 + 2 * tm * d * 4 + tm * d * 2 + d * d * 2
    return pl.pallas_call(
        _mix_kernel,
        grid=(t // tm,),
        in_specs=[pl.BlockSpec((tm, hw), lambda i: (i, 0)),
                  pl.BlockSpec((tm, hw), lambda i: (i, 0)),
                  pl.BlockSpec((tm, d), lambda i: (i, 0)),
                  pl.BlockSpec((hw, d), lambda i: (0, 0)),
                  pl.BlockSpec((hw, d), lambda i: (1, 0)),
                  pl.BlockSpec((1, d), lambda i: (0, 0)),
                  pl.BlockSpec((1, d), lambda i: (0, 0))],
        out_specs=[pl.BlockSpec((tm, d), lambda i: (i, 0)),
                   pl.BlockSpec((tm, d), lambda i: (i, 0))],
        out_shape=[jax.ShapeDtypeStruct((t, d), F32), jax.ShapeDtypeStruct((t, d), BF16)],
        compiler_params=pltpu.CompilerParams(
            dimension_semantics=("arbitrary",),
            vmem_limit_bytes=_vmem_limit(blocks)),
        name="mix",
    )(a_out, b_out, x2d, w_out_bf, w_out_bf, ln_g, ln_b)


def _ffn_kernel(xb_ref, xres_ref, wg_ref, wv_ref, cwg_ref, cwv_ref, cbg_ref, cbv_ref, wd_ref,
                g_ref, bt_ref, o_ref, ug_ref, uv_ref, tail_ref, h_ref,
                *, tm, tiles_per_seq, nf, ns):
    s = pl.program_id(0)
    su = jnp.minimum(s, ns - 1)
    iu = su // nf
    ju = su % nf
    jd = jnp.maximum(s - 1, 0) % nf
    halo = CONV_HALO

    @pl.when(s == 0)
    def _():
        tail_ref[...] = jnp.zeros(tail_ref.shape, F32)
        h_ref[...] = jnp.zeros(h_ref.shape, BF16)

    @pl.when(jd == 0)
    def _():
        o_ref[...] = jnp.zeros(o_ref.shape, F32)

    seq_start = (iu % tiles_per_seq) == 0

    def conv(w_ref, cw_ref, cb_ref, ubuf_ref, slot):
        u = _dot(xb_ref[...], w_ref[...])
        ubuf_ref[0:halo, :] = jnp.where(seq_start, 0.0, tail_ref[ju, slot])
        ubuf_ref[halo:halo + tm, :] = u
        u1 = ubuf_ref[halo - 1:halo - 1 + tm, :]
        u2 = ubuf_ref[halo - 2:halo - 2 + tm, :]
        tail_ref[ju, slot] = ubuf_ref[tm:tm + halo, :]
        cw = cw_ref[...]
        return cb_ref[...] + cw[0:1, :] * u2 + cw[1:2, :] * u1 + cw[2:3, :] * u

    def body(h_prev_ref, h_next_ref):
        gc = conv(wg_ref, cwg_ref, cbg_ref, ug_ref, 0)
        vc = conv(wv_ref, cwv_ref, cbv_ref, uv_ref, 1)
        o_ref[...] += _dot(h_prev_ref[...], wd_ref[...])
        h_next_ref[...] = ((gc * _sigmoid(gc)) * vc).astype(BF16)

    @pl.when(s % 2 == 0)
    def _():
        body(h_ref.at[1], h_ref.at[0])

    @pl.when(s % 2 == 1)
    def _():
        body(h_ref.at[0], h_ref.at[1])

    for c in range(o_ref.shape[1] // FFN_RES_COLS):
        @pl.when(jd == c)
        def _():
            cols = slice(c * FFN_RES_COLS, (c + 1) * FFN_RES_COLS)
            o_ref[:, cols] += ALPHA * xres_ref[...]

    @pl.when(jnp.logical_and(jd == nf - 1, s > 0))
    def _():
        o_ref[...] = _layer_norm(o_ref[...], g_ref[...], bt_ref[...])


def _ffn(x1, x1_bf, w_up_bf, conv_w, conv_b, w_down_bf, ln_g, ln_b, seq, tm=1024, fc=512):
    t, d = x1.shape
    nf = D_FF // fc
    ns = (t // tm) * nf
    n_res = d // FFN_RES_COLS
    assert n_res <= nf

    def up(s):
        return jnp.minimum(s, ns - 1)

    def down(s):
        return jnp.maximum(s - 1, 0)

    blocks = (tm * d * 2 + tm * FFN_RES_COLS * 4 + tm * d * 4 + 2 * d * fc * 2 + fc * d * 2
              + 8 * fc * 4)
    scratch = (2 * (tm + CONV_HALO) * fc * 4 + nf * 2 * CONV_HALO * fc * 4 + 2 * tm * fc * 2)
    return pl.pallas_call(
        functools.partial(_ffn_kernel, tm=tm, tiles_per_seq=seq // tm, nf=nf, ns=ns),
        grid=(ns + 1,),
        in_specs=[pl.BlockSpec((tm, d), lambda s: (up(s) // nf, 0)),
                  pl.BlockSpec((tm, FFN_RES_COLS),
                               lambda s: (down(s) // nf, jnp.minimum(down(s) % nf, n_res - 1))),
                  pl.BlockSpec((d, fc), lambda s: (0, up(s) % nf)),
                  pl.BlockSpec((d, fc), lambda s: (0, up(s) % nf + nf)),
                  pl.BlockSpec((CONV_WIDTH, fc), lambda s: (0, up(s) % nf)),
                  pl.BlockSpec((CONV_WIDTH, fc), lambda s: (0, up(s) % nf + nf)),
                  pl.BlockSpec((1, fc), lambda s: (0, up(s) % nf)),
                  pl.BlockSpec((1, fc), lambda s: (0, up(s) % nf + nf)),
                  pl.BlockSpec((fc, d), lambda s: (down(s) % nf, 0)),
                  pl.BlockSpec((1, d), lambda s: (0, 0)),
                  pl.BlockSpec((1, d), lambda s: (0, 0))],
        out_specs=pl.BlockSpec((tm, d), lambda s: (down(s) // nf, 0)),
        out_shape=jax.ShapeDtypeStruct((t, d), F32),
        scratch_shapes=[pltpu.VMEM((tm + CONV_HALO, fc), F32),
                        pltpu.VMEM((tm + CONV_HALO, fc), F32),
                        pltpu.VMEM((nf, 2, CONV_HALO, fc), F32),
                        pltpu.VMEM((2, tm, fc), BF16)],
        compiler_params=pltpu.CompilerParams(
            dimension_semantics=("arbitrary",),
            vmem_limit_bytes=_vmem_limit(blocks + scratch)),
        name="ffn",
    )(x1_bf, x1, w_up_bf, w_up_bf, conv_w, conv_w, conv_b, conv_b, w_down_bf, ln_g, ln_b)


def kernel(x, w_in, pool_w, pool_b, pool_scale, hgrn_lb_logits, hgrn_g_norm, w_out,
           ln1_g, ln1_b, w_up, conv_w, conv_b, w_down, ln2_g, ln2_b):
    bsz, seq, d = x.shape
    assert (d, w_in.shape[0], hgrn_lb_logits.shape[0]) == (D_MODEL, DEPTH, DEPTH + 1)
    t = bsz * seq
    x2d = x.reshape(t, d)

    u_pool, proj = _proj(x2d, w_in[0], hgrn_lb_logits)
    a_out = _pool(u_pool.reshape(bsz, seq, POOL_WIDTH), pool_w[0], pool_b[0],
                  pool_scale[0].reshape(1, POOL_WIDTH))
    b_out, (w_up_bf, w_down_bf, w_out_bf) = _hgrn(
        proj.reshape(bsz, seq, IN_COLS - POOL_WIDTH), hgrn_g_norm[0].reshape(1, HGRN_WIDTH),
        (w_up[0], w_down[0], w_out[0]))
    x1, x1_bf = _mix(a_out, b_out, x2d, w_out_bf, ln1_g[0].reshape(1, d), ln1_b[0].reshape(1, d))
    out = _ffn(x1, x1_bf, w_up_bf, conv_w[0], conv_b[0].reshape(1, 2 * D_FF),
               w_down_bf, ln2_g[0].reshape(1, d), ln2_b[0].reshape(1, d), seq)
    return out.reshape(bsz, seq, d)
```
